```python
import jax, jax.numpy as jnp
from jax import lax
import numpy as np

D_MODEL = 1024
BATCH = 1
SEQ = 16384
DEPTH = 2

CHUNK = 64
N_MEM = 256
ROPE_THETA = 10000.0
LN_EPS = 1e-5

POOL_GROUPS = 4
POOL_GROUP_DIM = 128
POOL_WINDOWS = (2, 4, 8, 16)
POOL_WIDTH = POOL_GROUPS * POOL_GROUP_DIM
ATT_HEADS = 8
ATT_HEAD_DIM = 64
ATT_WIDTH = ATT_HEADS * ATT_HEAD_DIM
IDX_HEADS = 8
IDX_HEAD_DIM = 64
TOPK_MAX = 256
Q_BLOCK = 128
MEM_HEADS = 4
MEM_HEAD_DIM = 128
MEM_WIDTH = MEM_HEADS * MEM_HEAD_DIM
N_BRANCH = 3
BRANCH_WIDTH = 512
IN_WIDTHS = (POOL_WIDTH, ATT_WIDTH, ATT_WIDTH, ATT_WIDTH, IDX_HEADS * IDX_HEAD_DIM,
             IDX_HEAD_DIM, IDX_HEADS, MEM_WIDTH, N_BRANCH * D_MODEL)
D_IN = sum(IN_WIDTHS)
N_EXPERTS = 16
N_GROUPS = 4
EXPERTS_PER_GROUP = N_EXPERTS // N_GROUPS
TOP_K_EXPERTS = 2
GROUP_SCORE_TOPK = 2
D_EXPERT = 512
MOE_BLOCK = 128
DN_ALPHA = (2 * DEPTH) ** 0.25
DN_BETA = (8 * DEPTH) ** -0.25

kernel_name = 'hybrid_pool_dsa_mem_moe_deepnorm'


def layer_norm(x, g, b):
    xf = x.astype(jnp.float32)
    mu = jnp.mean(xf, axis=-1, keepdims=True)
    var = jnp.mean(jnp.square(xf - mu), axis=-1, keepdims=True)
    y = (xf - mu) * lax.rsqrt(var + LN_EPS) * g.astype(jnp.float32) + b.astype(jnp.float32)
    return y.astype(x.dtype)


def rotary(x, positions):
    half = x.shape[-1] // 2
    inv_freq = ROPE_THETA ** (-jnp.arange(half, dtype=jnp.float32) / half)
    ang = positions.astype(jnp.float32)[..., None] * inv_freq
    cos = jnp.cos(ang)[:, :, None, :]
    sin = jnp.sin(ang)[:, :, None, :]
    xf = x.astype(jnp.float32)
    x1, x2 = xf[..., :half], xf[..., half:]
    out = jnp.concatenate([x1 * cos - x2 * sin, x2 * cos + x1 * sin], axis=-1)
    return out.astype(x.dtype)


def pool_branch(u, pool_w, pool_scale):
    B, S, _ = u.shape
    u = u.reshape(B, S, POOL_GROUPS, POOL_GROUP_DIM)
    t = jnp.arange(S)
    outs = []
    for g, w in enumerate(POOL_WINDOWS):
        ug = u[:, :, g].astype(jnp.float32)
        cs = jnp.cumsum(ug, axis=1)
        lagged = jnp.pad(cs, ((0, 0), (w, 0), (0, 0)))[:, :S]
        cnt = jnp.minimum(t + 1, w).astype(jnp.float32)[None, :, None]
        outs.append((cs - lagged) / cnt - ug)
    d = jnp.stack(outs, axis=2).astype(u.dtype)
    mixed = jnp.einsum('bsgc,gce->bsge', d, pool_w).reshape(B, S, POOL_WIDTH)
    return mixed * pool_scale


def dsa_branch(q, k, v, iq, ik, iw, positions):
    B, S, _ = q.shape
    q = rotary(q.reshape(B, S, ATT_HEADS, ATT_HEAD_DIM), positions)
    k = rotary(k.reshape(B, S, ATT_HEADS, ATT_HEAD_DIM), positions)
    v = v.reshape(B, S, ATT_HEADS, ATT_HEAD_DIM)
    iq = rotary(iq.reshape(B, S, IDX_HEADS, IDX_HEAD_DIM), positions)
    ik = rotary(ik[:, :, None, :], positions)[:, :, 0]
    iw = iw * (IDX_HEADS ** -0.5)
    n_sel = min(TOPK_MAX, S // 4)
    nb = S // Q_BLOCK
    key_chunk = jnp.arange(S) // CHUNK

    def to_blocks(a):
        return a.reshape(B, nb, Q_BLOCK, *a.shape[2:]).swapaxes(0, 1)

    def attend_block(args):
        blk, qb, iqb, iwb = args
        t = blk * Q_BLOCK + jnp.arange(Q_BLOCK)
        q_chunk = t // CHUNK
        admissible = key_chunk[None, :] <= q_chunk[:, None]
        rel = jax.nn.relu(jnp.einsum('bqhd,bsd->bqhs', iqb, ik) * (IDX_HEAD_DIM ** -0.5))
        score = jnp.einsum('bqhs,bqh->bqs', rel, iwb).astype(jnp.float32)
        score = jnp.where(admissible[None], score, -jnp.inf)
        _, sel = lax.top_k(score, n_sel)
        valid = (sel // CHUNK) <= q_chunk[None, :, None]
        kg = jax.vmap(lambda kk, ii: kk[ii])(k, sel)
        vg = jax.vmap(lambda vv, ii: vv[ii])(v, sel)
        s = jnp.einsum('bqhd,bqkhd->bqhk', qb, kg).astype(jnp.float32) * (ATT_HEAD_DIM ** -0.5)
        s = jnp.where(valid[:, :, None, :], s, -jnp.inf)
        p = jax.nn.softmax(s, axis=-1).astype(vg.dtype)
        return jnp.einsum('bqhk,bqkhd->bqhd', p, vg)

    out = lax.map(attend_block, (jnp.arange(nb), to_blocks(q), to_blocks(iq), to_blocks(iw)))
    return out.swapaxes(0, 1).reshape(B, S, ATT_WIDTH)


def memory_branch(mq, mem, w_mem_kv):
    B, S, _ = mq.shape
    M = mem.shape[1]
    kv = jnp.einsum('bmd,de->bme', mem, w_mem_kv)
    mk, mv = jnp.split(kv, 2, axis=-1)
    mk = mk.reshape(B, M, MEM_HEADS, MEM_HEAD_DIM)
    mv = mv.reshape(B, M, MEM_HEADS, MEM_HEAD_DIM)
    q = mq.reshape(B, S, MEM_HEADS, MEM_HEAD_DIM)
    s = jnp.einsum('bshd,bmhd->bhsm', q, mk).astype(jnp.float32) * (MEM_HEAD_DIM ** -0.5)
    p = jax.nn.softmax(s, axis=-1).astype(mv.dtype)
    return jnp.einsum('bhsm,bmhd->bshd', p, mv).reshape(B, S, MEM_WIDTH)


def token_mixing(x, mem, positions, w_in, b_in, pool_w, pool_scale, w_mem_kv, w_br, w_out):
    B, S, D = x.shape
    proj = jnp.einsum('bsd,de->bse', x, w_in) + b_in
    split_points = np.cumsum(IN_WIDTHS)[:-1].tolist()
    u_pool, q, k, v, iq, ik, iw, mq, gate_logits = jnp.split(proj, split_points, axis=-1)
    gates = jax.nn.sigmoid(gate_logits.astype(jnp.float32)).astype(x.dtype).reshape(B, S, N_BRANCH, D)
    branches = (pool_branch(u_pool, pool_w, pool_scale),
                dsa_branch(q, k, v, iq, ik, iw, positions),
                memory_branch(mq, mem, w_mem_kv))
    merged = jnp.zeros_like(x)
    for n in range(N_BRANCH):
        merged = merged + gates[:, :, n] * jnp.einsum('bsc,cd->bsd', branches[n], w_br[n])
    return jnp.einsum('bsd,de->bse', merged, w_out)


def route(x2, w_router, router_bias):
    scores = jax.nn.sigmoid(jnp.einsum('nd,de->ne', x2, w_router).astype(jnp.float32))
    biased = scores + router_bias.astype(jnp.float32)
    grouped = biased.reshape(-1, N_GROUPS, EXPERTS_PER_GROUP)
    group_score = lax.top_k(grouped, GROUP_SCORE_TOPK)[0].sum(-1)
    best = jnp.argmax(group_score, axis=-1).astype(jnp.int32)
    within = jnp.take_along_axis(grouped, best[:, None, None], axis=1)[:, 0]
    _, local = lax.top_k(within, TOP_K_EXPERTS)
    expert_idx = (best[:, None] * EXPERTS_PER_GROUP + local).astype(jnp.int32)
    w = jnp.take_along_axis(scores, expert_idx, axis=1)
    w = w / jnp.sum(w, axis=-1, keepdims=True)
    return expert_idx, w


def moe(x2, w_router, router_bias, w1, w3, w2):
    N, D = x2.shape
    expert_idx, gate_w = route(x2, w_router, router_bias)
    n_assign = N * TOP_K_EXPERTS
    n_blk = -(-n_assign // MOE_BLOCK) + N_EXPERTS
    flat_e = expert_idx.reshape(-1)
    flat_tok = jnp.repeat(jnp.arange(N, dtype=jnp.int32), TOP_K_EXPERTS)
    flat_g = gate_w.reshape(-1).astype(x2.dtype)
    order = jnp.argsort(flat_e)
    e_sorted = flat_e[order]
    counts = jax.ops.segment_sum(jnp.ones_like(flat_e), flat_e, num_segments=N_EXPERTS)
    padded = ((counts + MOE_BLOCK - 1) // MOE_BLOCK) * MOE_BLOCK
    pad_end = jnp.cumsum(padded)
    pad_start = pad_end - padded
    start = jnp.cumsum(counts) - counts
    rank = jnp.arange(n_assign, dtype=jnp.int32) - start[e_sorted]
    dest = pad_start[e_sorted] + rank
    slot_tok = jnp.full((n_blk * MOE_BLOCK,), N, jnp.int32).at[dest].set(flat_tok[order])
    slot_gate = jnp.zeros((n_blk * MOE_BLOCK,), x2.dtype).at[dest].set(flat_g[order])
    blk_e = jnp.minimum(jnp.searchsorted(pad_end, jnp.arange(n_blk) * MOE_BLOCK, side='right'),
                        N_EXPERTS - 1).astype(jnp.int32)
    x_pad = jnp.concatenate([x2, jnp.zeros((1, D), x2.dtype)], axis=0)

    def run_block(args):
        e, toks = args
        xb = x_pad[toks]
        h = jax.nn.silu(xb @ w1[e]) * (xb @ w3[e])
        return h @ w2[e]

    yb = lax.map(run_block, (blk_e, slot_tok.reshape(n_blk, MOE_BLOCK)))
    y = yb.reshape(-1, D) * slot_gate[:, None]
    return jnp.zeros((N + 1, D), x2.dtype).at[slot_tok].add(y)[:N]


def setup_inputs(seed: int = 0) -> dict:
    key = jax.random.key(seed)
    ks = jax.random.split(key, 20)
    f32 = jnp.float32
    nrm = lambda k, shape, scale: jax.random.normal(k, shape, f32) * scale
    return {
        'x': nrm(ks[0], (BATCH, SEQ, D_MODEL), 1.0),
        'mem': nrm(ks[1], (BATCH, N_MEM, D_MODEL), 1.0),
        'positions': jnp.broadcast_to(jnp.arange(SEQ, dtype=jnp.int32)[None], (BATCH, SEQ)),
        'w_in': nrm(ks[2], (DEPTH, D_MODEL, D_IN), D_MODEL ** -0.5),
        'b_in': nrm(ks[3], (DEPTH, D_IN), 0.02),
        'pool_w': nrm(ks[4], (DEPTH, POOL_GROUPS, POOL_GROUP_DIM, POOL_GROUP_DIM), POOL_GROUP_DIM ** -0.5),
        'pool_scale': 1.0 + nrm(ks[5], (DEPTH, POOL_WIDTH), 0.02),
        'w_mem_kv': nrm(ks[6], (DEPTH, D_MODEL, 2 * MEM_WIDTH), D_MODEL ** -0.5),
        'w_br': nrm(ks[7], (DEPTH, N_BRANCH, BRANCH_WIDTH, D_MODEL), DN_BETA * BRANCH_WIDTH ** -0.5),
        'w_out': nrm(ks[8], (DEPTH, D_MODEL, D_MODEL), DN_BETA * D_MODEL ** -0.5),
        'ln1_g': 1.0 + nrm(ks[9], (DEPTH, D_MODEL), 0.02),
        'ln1_b': nrm(ks[10], (DEPTH, D_MODEL), 0.02),
        'w_router': nrm(ks[11], (D_MODEL, N_EXPERTS), D_MODEL ** -0.5),
        'router_bias': nrm(ks[12], (N_EXPERTS,), 0.01),
        'w1': nrm(ks[13], (DEPTH, N_EXPERTS, D_MODEL, D_EXPERT), D_MODEL ** -0.5),
        'w3': nrm(ks[14], (DEPTH, N_EXPERTS, D_MODEL, D_EXPERT), D_MODEL ** -0.5),
        'w2': nrm(ks[15], (DEPTH, N_EXPERTS, D_EXPERT, D_MODEL), DN_BETA * D_EXPERT ** -0.5),
        'ln2_g': 1.0 + nrm(ks[16], (DEPTH, D_MODEL), 0.02),
        'ln2_b': nrm(ks[17], (DEPTH, D_MODEL), 0.02),
    }


def reference(x, mem, positions, w_in, b_in, pool_w, pool_scale, w_mem_kv, w_br, w_out,
              ln1_g, ln1_b, w_router, router_bias, w1, w3, w2, ln2_g, ln2_b):
    B, S, D = x.shape
    for l in range(DEPTH):
        mix = token_mixing(x, mem, positions, w_in[l], b_in[l], pool_w[l], pool_scale[l],
                           w_mem_kv[l], w_br[l], w_out[l])
        x = layer_norm(DN_ALPHA * x + mix, ln1_g[l], ln1_b[l])
        ffn = moe(x.reshape(B * S, D), w_router, router_bias, w1[l], w3[l], w2[l]).reshape(B, S, D)
        x = layer_norm(DN_ALPHA * x + ffn, ln2_g[l], ln2_b[l])
    return x
```

```python
import functools

import jax
import jax.numpy as jnp
from jax import lax
from jax.experimental import pallas as pl
from jax.experimental.pallas import tpu as pltpu

D_MODEL = 1024
DEPTH = 2
CHUNK = 64
ROPE_THETA = 10000.0
LN_EPS = 1e-5
POOL_WINDOWS = (2, 4, 8, 16)
POOL_GROUP_DIM = 128
POOL_HALO = 16
ATT_HEADS = 8
HEAD_DIM = 64
ROPE_HALF = HEAD_DIM // 2
IDX_HEADS = 8
TOPK_MAX = 256
MEM_HEADS = 4
MEM_HEAD_DIM = 128
BRANCH_WIDTH = 512
N_BRANCH = 3
N_EXPERTS = 16
N_GROUPS = 4
EXPERTS_PER_GROUP = 4
D_EXPERT = 512
DN_ALPHA = (2 * DEPTH) ** 0.25

LANES = 128
INT_MIN = -(2 ** 31)
NEG_BIG = -1e30

PROJ_TM = 512
DSA_TQ = 256
DSA_TK = 512
MIX_TM = 256
MOE_TM = 512
VMEM_LIMIT = 56 * 1024 * 1024

F32 = jnp.float32
BF16 = jnp.bfloat16


def _dot(a, b):
    return jnp.dot(a, b, preferred_element_type=F32)


def _dot_nt(a, b):
    return lax.dot_general(a, b, (((1,), (1,)), ((), ())), preferred_element_type=F32)


def _layer_norm(y, g, b):
    mu = jnp.mean(y, axis=-1, keepdims=True)
    yc = y - mu
    var = jnp.mean(yc * yc, axis=-1, keepdims=True)
    return yc * lax.rsqrt(var + LN_EPS) * g + b


def _sigmoid(x):
    return 1.0 / (1.0 + jnp.exp(-x))


def _matmul_kernel(a_ref, b_ref, o_ref):
    o_ref[...] = _dot(a_ref[...].astype(BF16), b_ref[...])


def _matmul(a, b):
    m, n = a.shape[0], b.shape[1]
    return pl.pallas_call(
        _matmul_kernel,
        out_shape=jax.ShapeDtypeStruct((m, n), F32),
        name="mem_kv",
    )(a, b)


_R_Q, _R_IQ, _R_K, _R_IK, _R_V, _R_IW, _R_END = 0, 512, 1024, 1536, 1600, 2112, 2128


def _proj_rope_kernel(x_ref, w_ref, b_ref, cos_ref, sin_ref,
                      qT_ref, iqT_ref, k_ref, ik_ref, vT_ref, iwT_ref, kT_scr, ikT_scr):
    xb = x_ref[...].astype(BF16)
    cos = cos_ref[...]
    sin = sin_ref[...]

    def proj(r0, r1):
        return _dot_nt(w_ref[r0:r1, :], xb) + b_ref[r0:r1, :]

    def rope_store(r, dst_ref, dtype):
        for h in range(r.shape[0] // HEAD_DIM):
            a = h * HEAD_DIM
            x1 = r[a:a + ROPE_HALF]
            x2 = r[a + ROPE_HALF:a + HEAD_DIM]
            dst_ref[a:a + ROPE_HALF, :] = (x1 * cos - x2 * sin).astype(dtype)
            dst_ref[a + ROPE_HALF:a + HEAD_DIM, :] = (x2 * cos + x1 * sin).astype(dtype)

    rope_store(proj(_R_Q, _R_IQ), qT_ref, BF16)
    rope_store(proj(_R_IQ, _R_K), iqT_ref, BF16)
    rope_store(proj(_R_K, _R_IK), kT_scr, F32)
    k_ref[...] = kT_scr[...].T.astype(BF16)
    ikT_scr[HEAD_DIM:, :] = jnp.zeros((LANES - HEAD_DIM, ikT_scr.shape[1]), F32)
    rope_store(proj(_R_IK, _R_V), ikT_scr, F32)
    ik_ref[...] = ikT_scr[...].T.astype(BF16)
    vT_ref[...] = proj(_R_V, _R_IW).astype(BF16)
    iwT_ref[...] = proj(_R_IW, _R_END)[0:IDX_HEADS]


def _proj_rope(x2, wT, bT, cosT, sinT):
    s = x2.shape[0]
    tm = min(PROJ_TM, s)
    col = lambda i: (0, i)
    return pl.pallas_call(
        _proj_rope_kernel,
        grid=(s // tm,),
        in_specs=[
            pl.BlockSpec((tm, D_MODEL), lambda i: (i, 0)),
            pl.BlockSpec((_R_END, D_MODEL), lambda i: (0, 0)),
            pl.BlockSpec((_R_END, 1), lambda i: (0, 0)),
            pl.BlockSpec((ROPE_HALF, tm), col),
            pl.BlockSpec((ROPE_HALF, tm), col),
        ],
        out_specs=[
            pl.BlockSpec((512, tm), col),
            pl.BlockSpec((512, tm), col),
            pl.BlockSpec((tm, 512), lambda i: (i, 0)),
            pl.BlockSpec((tm, LANES), lambda i: (i, 0)),
            pl.BlockSpec((512, tm), col),
            pl.BlockSpec((IDX_HEADS, tm), col),
        ],
        out_shape=[
            jax.ShapeDtypeStruct((512, s), BF16),
            jax.ShapeDtypeStruct((512, s), BF16),
            jax.ShapeDtypeStruct((s, 512), BF16),
            jax.ShapeDtypeStruct((s, LANES), BF16),
            jax.ShapeDtypeStruct((512, s), BF16),
            jax.ShapeDtypeStruct((IDX_HEADS, s), F32),
        ],
        scratch_shapes=[pltpu.VMEM((512, tm), F32), pltpu.VMEM((LANES, tm), F32)],
        compiler_params=pltpu.CompilerParams(
            dimension_semantics=("parallel",), vmem_limit_bytes=VMEM_LIMIT),
        name="proj_rope",
    )(x2, wT, bT, cosT, sinT)


def _dsa_kernel(qT_ref, iqT_ref, iwT_ref, ik_ref, k_hbm, vT_hbm, o_ref,
                sc_ref, kbuf, vbuf, sem, iqp_ref, qp_ref, m_ref, l_ref, acc_ref, *, n_sel):
    tq = qT_ref.shape[1]
    tk = kbuf.shape[1]
    i = pl.program_id(0)
    q0 = i * tq
    n_kt = (q0 + tq + tk - 1) // tk

    zpad = jnp.zeros((LANES - HEAD_DIM, tq), BF16)
    for h in range(IDX_HEADS):
        iqp_ref[h, 0:HEAD_DIM, :] = iqT_ref[h * HEAD_DIM:(h + 1) * HEAD_DIM, :]
        iqp_ref[h, HEAD_DIM:, :] = zpad
    row = lax.broadcasted_iota(jnp.int32, (256, tq), 0)
    for h in range(ATT_HEADS):
        j, hh = h // 4, h % 4
        quad = qT_ref[256 * j:256 * (j + 1), :]
        keep = (row >= hh * HEAD_DIM) & (row < (hh + 1) * HEAD_DIM)
        qp_ref[h] = jnp.where(keep, quad, jnp.zeros_like(quad))

    q_chunk = (q0 + lax.broadcasted_iota(jnp.int32, (1, tq), 1)) // CHUNK

    def score_tile(kt, carry):
        ks = pl.multiple_of(kt * tk, tk)
        ikt = ik_ref[pl.ds(ks, tk), :]
        acc = jnp.zeros((tk, tq), F32)
        for h in range(IDX_HEADS):
            z = _dot(ikt, iqp_ref[h])
            acc = acc + jnp.maximum(z, 0.0) * iwT_ref[h:h + 1, :]
        acc = acc + 0.0
        bits = pltpu.bitcast(acc, jnp.int32)
        skey = bits ^ ((bits >> 31) & jnp.int32(0x7FFFFFFF))
        key_chunk = (ks + lax.broadcasted_iota(jnp.int32, (tk, 1), 0)) // CHUNK
        skey = jnp.where(key_chunk <= q_chunk, skey, jnp.int32(INT_MIN))
        sc_ref[pl.ds(ks, tk), :] = skey
        return carry

    lax.fori_loop(0, n_kt, score_tile, 0)

    def count_ge(cand_s):
        def body(kt, c):
            ks = pl.multiple_of(kt * tk, tk)
            ge = (sc_ref[pl.ds(ks, tk), :] >= cand_s).astype(jnp.int32)
            return c + jnp.sum(ge.reshape(tk // 8, 8, tq), axis=0)
        c = lax.fori_loop(0, n_kt, body, jnp.zeros((8, tq), jnp.int32))
        return jnp.sum(c, axis=0, keepdims=True)

    def bisect(b, ans_u):
        cand_u = ans_u | (jnp.int32(1) << (31 - b))
        cnt = count_ge(cand_u ^ jnp.int32(INT_MIN))
        return jnp.where(cnt >= n_sel, cand_u, ans_u)

    ans_u = lax.fori_loop(0, 32, bisect, jnp.zeros((1, tq), jnp.int32))
    tau = jnp.maximum(ans_u ^ jnp.int32(INT_MIN), jnp.int32(INT_MIN + 1))

    def k_copy(kt, slot):
        return pltpu.make_async_copy(k_hbm.at[pl.ds(kt * tk, tk), :], kbuf.at[slot], sem.at[0, slot])

    def v_copy(kt, slot):
        return pltpu.make_async_copy(vT_hbm.at[:, pl.ds(kt * tk, tk)], vbuf.at[slot], sem.at[1, slot])

    m_ref[...] = jnp.full(m_ref.shape, NEG_BIG, F32)
    l_ref[...] = jnp.zeros(l_ref.shape, F32)
    acc_ref[...] = jnp.zeros(acc_ref.shape, F32)
    k_copy(0, 0).start()
    v_copy(0, 0).start()

    def attend_tile(kt, carry):
        slot = kt % 2
        k_copy(kt, slot).wait()
        v_copy(kt, slot).wait()

        @pl.when(kt + 1 < n_kt)
        def _():
            k_copy(kt + 1, 1 - slot).start()
            v_copy(kt + 1, 1 - slot).start()

        ks = pl.multiple_of(kt * tk, tk)
        sel = sc_ref[pl.ds(ks, tk), :] >= tau
        for h in range(ATT_HEADS):
            j = h // 4
            s = _dot(kbuf[slot, :, 256 * j:256 * (j + 1)], qp_ref[h])
            s = jnp.where(sel, s, NEG_BIG)
            m_old = m_ref[h:h + 1, :]
            m_new = jnp.maximum(m_old, jnp.max(s, axis=0, keepdims=True))
            alpha = jnp.exp(m_old - m_new)
            p = jnp.where(sel, jnp.exp(s - m_new), 0.0)
            l_ref[h:h + 1, :] = alpha * l_ref[h:h + 1, :] + jnp.sum(p, axis=0, keepdims=True)
            pv = _dot(vbuf[slot, h * HEAD_DIM:(h + 1) * HEAD_DIM, :], p.astype(BF16))
            a0 = h * HEAD_DIM
            acc_ref[a0:a0 + HEAD_DIM, :] = alpha * acc_ref[a0:a0 + HEAD_DIM, :] + pv
            m_ref[h:h + 1, :] = m_new
        return carry

    lax.fori_loop(0, n_kt, attend_tile, 0)

    for h in range(ATT_HEADS):
        a0 = h * HEAD_DIM
        acc_ref[a0:a0 + HEAD_DIM, :] = acc_ref[a0:a0 + HEAD_DIM, :] / l_ref[h:h + 1, :]
    o_ref[...] = acc_ref[...].T.astype(BF16)


def _dsa(qT, iqT, iwT, ik, k, vT):
    s = k.shape[0]
    tq = min(DSA_TQ, s)
    tk = min(DSA_TK, s)
    n_sel = min(TOPK_MAX, s // 4)
    col = lambda i: (0, i)
    return pl.pallas_call(
        functools.partial(_dsa_kernel, n_sel=n_sel),
        grid=(s // tq,),
        in_specs=[
            pl.BlockSpec((512, tq), col),
            pl.BlockSpec((512, tq), col),
            pl.BlockSpec((IDX_HEADS, tq), col),
            pl.BlockSpec((s, LANES), lambda i: (0, 0)),
            pl.BlockSpec(memory_space=pl.ANY),
            pl.BlockSpec(memory_space=pl.ANY),
        ],
        out_specs=pl.BlockSpec((tq, 512), lambda i: (i, 0)),
        out_shape=jax.ShapeDtypeStruct((s, 512), BF16),
        scratch_shapes=[
            pltpu.VMEM((s, tq), jnp.int32),
            pltpu.VMEM((2, tk, 512), BF16),
            pltpu.VMEM((2, 512, tk), BF16),
            pltpu.SemaphoreType.DMA((2, 2)),
            pltpu.VMEM((IDX_HEADS, LANES, tq), BF16),
            pltpu.VMEM((ATT_HEADS, 256, tq), BF16),
            pltpu.VMEM((ATT_HEADS, tq), F32),
            pltpu.VMEM((ATT_HEADS, tq), F32),
            pltpu.VMEM((512, tq), F32),
        ],
        compiler_params=pltpu.CompilerParams(
            dimension_semantics=("arbitrary",), vmem_limit_bytes=VMEM_LIMIT),
        name="dsa",
    )(qT, iqT, iwT, ik, k, vT)


_C_POOL, _C_MQ, _C_GATE = 0, 512, 1024


def _route(logitsT, bias):
    sc = _sigmoid(logitsT)
    bi = sc + bias
    s_rows = [sc[e:e + 1, :] for e in range(N_EXPERTS)]
    b_rows = [bi[e:e + 1, :] for e in range(N_EXPERTS)]
    best = None
    best_score = None
    for g in range(N_GROUPS):
        v = b_rows[4 * g:4 * g + 4]
        top2 = None
        for a in range(4):
            for c in range(a + 1, 4):
                pair = v[a] + v[c]
                top2 = pair if top2 is None else jnp.maximum(top2, pair)
        if best is None:
            best, best_score = jnp.zeros_like(top2, dtype=jnp.int32), top2
        else:
            upd = top2 > best_score
            best = jnp.where(upd, g, best)
            best_score = jnp.where(upd, top2, best_score)
    wb, ws = [], []
    for m in range(EXPERTS_PER_GROUP):
        vb, vs = b_rows[m], s_rows[m]
        for g in range(1, N_GROUPS):
            vb = jnp.where(best == g, b_rows[4 * g + m], vb)
            vs = jnp.where(best == g, s_rows[4 * g + m], vs)
        wb.append(vb)
        ws.append(vs)
    i1, v1 = jnp.zeros_like(best), wb[0]
    for m in range(1, 4):
        upd = wb[m] > v1
        i1 = jnp.where(upd, m, i1)
        v1 = jnp.where(upd, wb[m], v1)
    i2, v2 = None, None
    for m in range(4):
        cand = jnp.where(i1 == m, -jnp.inf, wb[m])
        if i2 is None:
            i2, v2 = jnp.zeros_like(best), cand
        else:
            upd = cand > v2
            i2 = jnp.where(upd, m, i2)
            v2 = jnp.where(upd, cand, v2)
    s1 = ws[0]
    s2 = ws[0]
    for m in range(1, 4):
        s1 = jnp.where(i1 == m, ws[m], s1)
        s2 = jnp.where(i2 == m, ws[m], s2)
    tot = s1 + s2
    g1, g2 = s1 / tot, s2 / tot
    rows = []
    for e in range(N_EXPERTS):
        g, m = e // 4, e % 4
        val = jnp.where(i1 == m, g1, jnp.where(i2 == m, g2, 0.0))
        rows.append(jnp.where(best == g, val, 0.0))
    return rows


def _mix_kernel(x_ref, xh_ref, dsa_ref, w_ref, b_ref, pw_ref, ps_ref, mk_ref, mv_ref,
                wbr_ref, wout_ref, g_ref, beta_ref, wr_ref, rb_ref,
                x1_ref, gate_ref, uext_ref):
    i = pl.program_id(0)
    tm = x_ref.shape[0]
    x = x_ref[...]
    xb = x.astype(BF16)

    def proj(c0, c1):
        return _dot(xb, w_ref[:, c0:c1]) + b_ref[:, c0:c1]

    u = proj(_C_POOL, _C_MQ)
    uh = _dot(xh_ref[...].astype(BF16), w_ref[:, _C_POOL:_C_MQ]) + b_ref[:, _C_POOL:_C_MQ]
    uext_ref[0:POOL_HALO, :] = jnp.where(i > 0, uh, 0.0)
    uext_ref[POOL_HALO:, :] = u
    t = i * tm + lax.broadcasted_iota(jnp.int32, (tm, 1), 0)
    pooled = []
    for g, w in enumerate(POOL_WINDOWS):
        c0 = g * POOL_GROUP_DIM
        ug = u[:, c0:c0 + POOL_GROUP_DIM]
        win = ug
        for j in range(1, w):
            win = win + uext_ref[POOL_HALO - j:POOL_HALO - j + tm, c0:c0 + POOL_GROUP_DIM]
        cnt = jnp.minimum(t + 1, w).astype(F32)
        d = win / cnt - ug
        pooled.append(_dot(d.astype(BF16), pw_ref[g]))
    pool_out = jnp.concatenate(pooled, axis=-1) * ps_ref[...]

    mq = proj(_C_MQ, _C_GATE)
    mem = []
    for h in range(MEM_HEADS):
        c0 = h * MEM_HEAD_DIM
        s = _dot_nt(mq[:, c0:c0 + MEM_HEAD_DIM].astype(BF16), mk_ref[:, c0:c0 + MEM_HEAD_DIM])
        s = s * (MEM_HEAD_DIM ** -0.5)
        s = s - jnp.max(s, axis=-1, keepdims=True)
        p = jnp.exp(s)
        p = p / jnp.sum(p, axis=-1, keepdims=True)
        mem.append(_dot(p.astype(BF16), mv_ref[:, c0:c0 + MEM_HEAD_DIM]))
    mem_out = jnp.concatenate(mem, axis=-1)

    merged = jnp.zeros((tm, D_MODEL), F32)
    for n, br in enumerate((pool_out.astype(BF16), dsa_ref[...], mem_out.astype(BF16))):
        gate = _sigmoid(proj(_C_GATE + n * D_MODEL, _C_GATE + (n + 1) * D_MODEL))
        merged = merged + gate * _dot(br, wbr_ref[n])
    mix = _dot(merged.astype(BF16), wout_ref[...])
    x1 = _layer_norm(DN_ALPHA * x + mix, g_ref[...], beta_ref[...])
    x1_ref[...] = x1

    logitsT = _dot_nt(wr_ref[...], x1.astype(BF16))
    rows = _route(logitsT, rb_ref[...])
    gT = jnp.concatenate(rows + [jnp.zeros((LANES - N_EXPERTS, tm), F32)], axis=0)
    gate_ref[...] = gT.T


def _mix(x2, dsa, w, b, pw, ps, mk, mv, wbr, wout, g, beta, wr, rb):
    s = x2.shape[0]
    tm = min(MIX_TM, s)
    const2 = lambda i: (0, 0)
    const3 = lambda i: (0, 0, 0)
    hb = tm // POOL_HALO
    return pl.pallas_call(
        _mix_kernel,
        grid=(s // tm,),
        in_specs=[
            pl.BlockSpec((tm, D_MODEL), lambda i: (i, 0)),
            pl.BlockSpec((POOL_HALO, D_MODEL), lambda i: (jnp.maximum(i * hb - 1, 0), 0)),
            pl.BlockSpec((tm, BRANCH_WIDTH), lambda i: (i, 0)),
            pl.BlockSpec(w.shape, const2),
            pl.BlockSpec(b.shape, const2),
            pl.BlockSpec(pw.shape, const3),
            pl.BlockSpec(ps.shape, const2),
            pl.BlockSpec(mk.shape, const2),
            pl.BlockSpec(mv.shape, const2),
            pl.BlockSpec(wbr.shape, const3),
            pl.BlockSpec(wout.shape, const2),
            pl.BlockSpec(g.shape, const2),
            pl.BlockSpec(beta.shape, const2),
            pl.BlockSpec(wr.shape, const2),
            pl.BlockSpec(rb.shape, const2),
        ],
        out_specs=[
            pl.BlockSpec((tm, D_MODEL), lambda i: (i, 0)),
            pl.BlockSpec((tm, LANES), lambda i: (i, 0)),
        ],
        out_shape=[
            jax.ShapeDtypeStruct((s, D_MODEL), F32),
            jax.ShapeDtypeStruct((s, LANES), F32),
        ],
        scratch_shapes=[pltpu.VMEM((POOL_HALO + tm, BRANCH_WIDTH), F32)],
        compiler_params=pltpu.CompilerParams(
            dimension_semantics=("parallel",), vmem_limit_bytes=VMEM_LIMIT),
        name="mix",
    )(x2, x2, dsa, w, b, pw, ps, mk, mv, wbr, wout, g, beta, wr, rb)


def _moe_kernel(x_ref, gate_ref, w1_ref, w3_ref, w2_ref, g_ref, beta_ref, o_ref, xb_ref, acc_ref):
    e = pl.program_id(1)

    @pl.when(e == 0)
    def _():
        xb_ref[...] = x_ref[...].astype(BF16)
        acc_ref[...] = jnp.zeros(acc_ref.shape, F32)

    xb = xb_ref[...]
    h1 = _dot(xb, w1_ref[0])
    h3 = _dot(xb, w3_ref[0])
    h = (h1 * _sigmoid(h1)) * h3
    y = _dot(h.astype(BF16), w2_ref[0])
    lane = lax.broadcasted_iota(jnp.int32, gate_ref.shape, 1)
    ge = jnp.sum(jnp.where(lane == e, gate_ref[...], 0.0), axis=-1, keepdims=True)
    acc_ref[...] += y * ge

    @pl.when(e == N_EXPERTS - 1)
    def _():
        o_ref[...] = _layer_norm(DN_ALPHA * x_ref[...] + acc_ref[...], g_ref[...], beta_ref[...])


def _moe(x1, gates, w1, w3, w2, g, beta):
    s = x1.shape[0]
    tm = min(MOE_TM, s)
    const2 = lambda i, e: (0, 0)
    return pl.pallas_call(
        _moe_kernel,
        grid=(s // tm, N_EXPERTS),
        in_specs=[
            pl.BlockSpec((tm, D_MODEL), lambda i, e: (i, 0)),
            pl.BlockSpec((tm, LANES), lambda i, e: (i, 0)),
            pl.BlockSpec((1, D_MODEL, D_EXPERT), lambda i, e: (e, 0, 0)),
            pl.BlockSpec((1, D_MODEL, D_EXPERT), lambda i, e: (e, 0, 0)),
            pl.BlockSpec((1, D_EXPERT, D_MODEL), lambda i, e: (e, 0, 0)),
            pl.BlockSpec(g.shape, const2),
            pl.BlockSpec(beta.shape, const2),
        ],
        out_specs=pl.BlockSpec((tm, D_MODEL), lambda i, e: (i, 0)),
        out_shape=jax.ShapeDtypeStruct((s, D_MODEL), F32),
        scratch_shapes=[pltpu.VMEM((tm, D_MODEL), BF16), pltpu.VMEM((tm, D_MODEL), F32)],
        compiler_params=pltpu.CompilerParams(
            dimension_semantics=("parallel", "arbitrary"), vmem_limit_bytes=VMEM_LIMIT),
        name="moe",
    )(x1, gates, w1, w3, w2, g, beta)


def _split_w_in(w_in, b_in):
    o = [0, 512, 1024, 1536, 2048, 2560, 2624, 2632, 3144, 6216]
    pool, q, k, v, iq, ik, iw, mq, gates = [(w_in[:, a:c], b_in[a:c]) for a, c in zip(o[:-1], o[1:])]
    qs = HEAD_DIM ** -0.5
    iws = (HEAD_DIM ** -0.5) * (IDX_HEADS ** -0.5)
    pad = _R_END - _R_IW - IDX_HEADS
    wT = jnp.concatenate([q[0] * qs, iq[0], k[0], ik[0], v[0], iw[0] * iws,
                          jnp.zeros((D_MODEL, pad), F32)], axis=1).T
    bT = jnp.concatenate([q[1] * qs, iq[1], k[1], ik[1], v[1], iw[1] * iws, jnp.zeros((pad,), F32)])
    w_mix = jnp.concatenate([pool[0], mq[0], gates[0]], axis=1)
    b_mix = jnp.concatenate([pool[1], mq[1], gates[1]])
    return wT.astype(BF16), bT[:, None], w_mix.astype(BF16), b_mix[None, :]


def kernel(x, mem, positions, w_in, b_in, pool_w, pool_scale, w_mem_kv, w_br, w_out, ln1_g, ln1_b,
           w_router, router_bias, w1, w3, w2, ln2_g, ln2_b):
    B, S, D = x.shape
    inv_freq = ROPE_THETA ** (-jnp.arange(ROPE_HALF, dtype=F32) / ROPE_HALF)
    wr = w_router.T.astype(BF16)
    rb = router_bias.astype(F32)[:, None]
    outs = []
    for bi in range(B):
        xs = x[bi]
        ang = positions[bi].astype(F32)[None, :] * inv_freq[:, None]
        cosT, sinT = jnp.cos(ang), jnp.sin(ang)
        for l in range(DEPTH):
            wT, bT, w_mix, b_mix = _split_w_in(w_in[l], b_in[l])
            kv = _matmul(mem[bi], w_mem_kv[l].astype(BF16))
            mk = kv[:, :BRANCH_WIDTH].astype(BF16)
            mv = kv[:, BRANCH_WIDTH:].astype(BF16)
            qT, iqT, k, ik, vT, iwT = _proj_rope(xs, wT, bT, cosT, sinT)
            dsa = _dsa(qT, iqT, iwT, ik, k, vT)
            x1, gates = _mix(xs, dsa, w_mix, b_mix, pool_w[l].astype(BF16), pool_scale[l][None, :],
                             mk, mv, w_br[l].astype(BF16), w_out[l].astype(BF16),
                             ln1_g[l][None, :], ln1_b[l][None, :], wr, rb)
            xs = _moe(x1, gates, w1[l].astype(BF16), w3[l].astype(BF16), w2[l].astype(BF16),
                      ln2_g[l][None, :], ln2_b[l][None, :])
        outs.append(xs)
    return jnp.stack(outs, axis=0)
```

```python
import functools

import jax
import jax.numpy as jnp
from jax import lax
from jax.experimental import pallas as pl
from jax.experimental.pallas import tpu as pltpu

D_MODEL = 1024
DEPTH = 2
CHUNK = 64
ROPE_THETA = 10000.0
LN_EPS = 1e-5
POOL_WINDOWS = (2, 4, 8, 16)
POOL_GROUP_DIM = 128
POOL_HALO = 16
ATT_HEADS = 8
HEAD_DIM = 64
ROPE_HALF = HEAD_DIM // 2
V_ROWS = HEAD_DIM + 16
LOG2_E = 1.4426950408889634
IDX_HEADS = 8
TOPK_MAX = 256
MEM_HEADS = 4
MEM_HEAD_DIM = 128
BRANCH_WIDTH = 512
N_BRANCH = 3
N_EXPERTS = 16
N_GROUPS = 4
EXPERTS_PER_GROUP = 4
D_EXPERT = 512
DN_ALPHA = (2 * DEPTH) ** 0.25

LANES = 128
INT_MIN = -(2 ** 31)
NEG_BIG = -1e30

PROJ_TM = 512
DSA_TQ = 256
DSA_TK = 512
MIX_TM = 256
MOE_TM = 512
VMEM_LIMIT = 56 * 1024 * 1024

F32 = jnp.float32
BF16 = jnp.bfloat16


def _dot(a, b):
    return jnp.dot(a, b, preferred_element_type=F32)


def _dot_nt(a, b):
    return lax.dot_general(a, b, (((1,), (1,)), ((), ())), preferred_element_type=F32)


def _layer_norm(y, g, b):
    mu = jnp.mean(y, axis=-1, keepdims=True)
    yc = y - mu
    var = jnp.mean(yc * yc, axis=-1, keepdims=True)
    return yc * lax.rsqrt(var + LN_EPS) * g + b


def _sigmoid(x):
    return 1.0 / (1.0 + jnp.exp(-x))


def _matmul_kernel(a_ref, b_ref, o_ref):
    o_ref[...] = _dot(a_ref[...].astype(BF16), b_ref[...])


def _matmul(a, b):
    m, n = a.shape[0], b.shape[1]
    return pl.pallas_call(
        _matmul_kernel,
        out_shape=jax.ShapeDtypeStruct((m, n), F32),
        name="mem_kv",
    )(a, b)


_R_Q, _R_IQ, _R_K, _R_IK, _R_V, _R_IW, _R_END = 0, 512, 1024, 1536, 1600, 2112, 2128


def _proj_rope_kernel(x_ref, w_ref, b_ref, cos_ref, sin_ref,
                      qT_ref, iqT_ref, k_ref, ik_ref, vT_ref, iwT_ref, kT_scr, ikT_scr):
    xb = x_ref[...].astype(BF16)
    cos = cos_ref[...]
    sin = sin_ref[...]

    def proj(r0, r1):
        return _dot_nt(w_ref[r0:r1, :], xb) + b_ref[r0:r1, :]

    def rope_store(r, dst_ref, dtype):
        for h in range(r.shape[0] // HEAD_DIM):
            a = h * HEAD_DIM
            x1 = r[a:a + ROPE_HALF]
            x2 = r[a + ROPE_HALF:a + HEAD_DIM]
            dst_ref[a:a + ROPE_HALF, :] = (x1 * cos - x2 * sin).astype(dtype)
            dst_ref[a + ROPE_HALF:a + HEAD_DIM, :] = (x2 * cos + x1 * sin).astype(dtype)

    rope_store(proj(_R_Q, _R_IQ), qT_ref, BF16)
    rope_store(proj(_R_IQ, _R_K), iqT_ref, BF16)
    rope_store(proj(_R_K, _R_IK), kT_scr, F32)
    k_ref[...] = kT_scr[...].T.astype(BF16)
    ikT_scr[HEAD_DIM:, :] = jnp.zeros((LANES - HEAD_DIM, ikT_scr.shape[1]), F32)
    rope_store(proj(_R_IK, _R_V), ikT_scr, F32)
    ik_ref[...] = ikT_scr[...].T.astype(BF16)
    v = proj(_R_V, _R_IW).astype(BF16)
    tm = v.shape[1]
    pad_row = lax.broadcasted_iota(jnp.int32, (V_ROWS - HEAD_DIM, tm), 0)
    ones_then_zeros = jnp.where(pad_row == 0, 1.0, 0.0).astype(BF16)
    for h in range(ATT_HEADS):
        a0 = h * V_ROWS
        vT_ref[a0:a0 + HEAD_DIM, :] = v[h * HEAD_DIM:(h + 1) * HEAD_DIM]
        vT_ref[a0 + HEAD_DIM:a0 + V_ROWS, :] = ones_then_zeros
    iwT_ref[...] = proj(_R_IW, _R_END)[0:IDX_HEADS]


def _proj_rope(x2, wT, bT, cosT, sinT):
    s = x2.shape[0]
    tm = min(PROJ_TM, s)
    col = lambda i: (0, i)
    return pl.pallas_call(
        _proj_rope_kernel,
        grid=(s // tm,),
        in_specs=[
            pl.BlockSpec((tm, D_MODEL), lambda i: (i, 0)),
            pl.BlockSpec((_R_END, D_MODEL), lambda i: (0, 0)),
            pl.BlockSpec((_R_END, 1), lambda i: (0, 0)),
            pl.BlockSpec((ROPE_HALF, tm), col),
            pl.BlockSpec((ROPE_HALF, tm), col),
        ],
        out_specs=[
            pl.BlockSpec((512, tm), col),
            pl.BlockSpec((512, tm), col),
            pl.BlockSpec((tm, 512), lambda i: (i, 0)),
            pl.BlockSpec((tm, LANES), lambda i: (i, 0)),
            pl.BlockSpec((ATT_HEADS * V_ROWS, tm), col),
            pl.BlockSpec((IDX_HEADS, tm), col),
        ],
        out_shape=[
            jax.ShapeDtypeStruct((512, s), BF16),
            jax.ShapeDtypeStruct((512, s), BF16),
            jax.ShapeDtypeStruct((s, 512), BF16),
            jax.ShapeDtypeStruct((s, LANES), BF16),
            jax.ShapeDtypeStruct((ATT_HEADS * V_ROWS, s), BF16),
            jax.ShapeDtypeStruct((IDX_HEADS, s), F32),
        ],
        scratch_shapes=[pltpu.VMEM((512, tm), F32), pltpu.VMEM((LANES, tm), F32)],
        compiler_params=pltpu.CompilerParams(
            dimension_semantics=("parallel",), vmem_limit_bytes=VMEM_LIMIT),
        name="proj_rope",
    )(x2, wT, bT, cosT, sinT)


def _dsa_kernel(qT_ref, iqT_ref, iwT_ref, ik_ref, k_hbm, vT_hbm, o_ref,
                sc_ref, kbuf, vbuf, sem, iqp_ref, qp_ref, m_ref, tmx_ref, acc_ref, out_ref,
                sa_ref, sb_ref, *, n_sel):
    tq = qT_ref.shape[1]
    tk = kbuf.shape[2]
    i = pl.program_id(0)
    q0 = i * tq
    n_kt = (q0 + tq + tk - 1) // tk

    zpad = jnp.zeros((LANES - HEAD_DIM, tq), BF16)
    for h in range(IDX_HEADS):
        iqp_ref[h, 0:HEAD_DIM, :] = iqT_ref[h * HEAD_DIM:(h + 1) * HEAD_DIM, :]
        iqp_ref[h, HEAD_DIM:, :] = zpad
    row = lax.broadcasted_iota(jnp.int32, (256, tq), 0)
    for h in range(ATT_HEADS):
        j, hh = h // 4, h % 4
        quad = qT_ref[256 * j:256 * (j + 1), :]
        keep = (row >= hh * HEAD_DIM) & (row < (hh + 1) * HEAD_DIM)
        qp_ref[h] = jnp.where(keep, quad, jnp.zeros_like(quad))

    q_chunk = (q0 + lax.broadcasted_iota(jnp.int32, (1, tq), 1)) // CHUNK

    def score_tile(kt, carry):
        ks = pl.multiple_of(kt * tk, tk)
        ikt = ik_ref[pl.ds(ks, tk), :]
        acc = jnp.zeros((tk, tq), F32)
        for h in range(IDX_HEADS):
            z = _dot(ikt, iqp_ref[h])
            acc = acc + jnp.maximum(z, 0.0) * iwT_ref[h:h + 1, :]
        acc = acc + 0.0
        bits = pltpu.bitcast(acc, jnp.int32)
        skey = bits ^ ((bits >> 31) & jnp.int32(0x7FFFFFFF))
        key_chunk = (ks + lax.broadcasted_iota(jnp.int32, (tk, 1), 0)) // CHUNK
        skey = jnp.where(key_chunk <= q_chunk, skey, jnp.int32(INT_MIN))
        sc_ref[pl.ds(ks, tk), :] = skey
        return carry

    lax.fori_loop(0, n_kt, score_tile, 0)

    def count_ge(cand_s):
        def body(kt, c):
            ks = pl.multiple_of(kt * tk, tk)
            ge = (sc_ref[pl.ds(ks, tk), :] >= cand_s).astype(jnp.int32)
            return c + jnp.sum(ge.reshape(tk // 8, 8, tq), axis=0)
        c = lax.fori_loop(0, n_kt, body, jnp.zeros((8, tq), jnp.int32))
        return jnp.sum(c, axis=0, keepdims=True)

    def bisect(b, ans_u):
        cand_u = ans_u | (jnp.int32(1) << (31 - b))
        cnt = count_ge(cand_u ^ jnp.int32(INT_MIN))
        return jnp.where(cnt >= n_sel, cand_u, ans_u)

    ans_u = lax.fori_loop(0, 32, bisect, jnp.zeros((1, tq), jnp.int32))
    tau = jnp.maximum(ans_u ^ jnp.int32(INT_MIN), jnp.int32(INT_MIN + 1))

    n2 = (n_kt + 1) // 2
    last = 2 * n2 - 1

    @pl.when(n_kt % 2 == 1)
    def _():
        sc_ref[pl.ds(pl.multiple_of(n_kt * tk, tk), tk), :] = jnp.full((tk, tq), INT_MIN, jnp.int32)

    def k_copy(t, par, w):
        return pltpu.make_async_copy(k_hbm.at[pl.ds(t * tk, tk), :], kbuf.at[par, w], sem.at[0, par, w])

    def v_copy(t, par, w):
        return pltpu.make_async_copy(vT_hbm.at[:, pl.ds(t * tk, tk)], vbuf.at[par, w], sem.at[1, par, w])

    def pair_copies(j, par):
        return (k_copy(2 * j + 1, par, 0), k_copy(jnp.minimum(2 * j + 2, last), par, 1),
                v_copy(2 * j, par, 0), v_copy(2 * j + 1, par, 1))

    def logits_head(kb, h, bias, s_dst, tmx_dst):
        j = h // 4
        s = _dot(kb[:, 256 * j:256 * (j + 1)], qp_ref[h]) + bias
        s_dst[h] = s
        tmx_dst[h:h + 1, :] = jnp.max(s, axis=0, keepdims=True)

    def mask_bias(t):
        ks = pl.multiple_of(t * tk, tk)
        return jnp.where(sc_ref[pl.ds(ks, tk), :] >= tau, 0.0, NEG_BIG)

    def attend_head(vb, h, s_src, tmx_src):
        m_old = m_ref[h:h + 1, :]
        m_new = jnp.maximum(m_old, tmx_src[h:h + 1, :])
        alpha = jnp.exp2(m_old - m_new)
        p = jnp.exp2(s_src[h] - m_new)
        a0 = h * V_ROWS
        pv = _dot(vb[a0:a0 + V_ROWS, :], p.astype(BF16))
        acc_ref[a0:a0 + V_ROWS, :] = alpha * acc_ref[a0:a0 + V_ROWS, :] + pv
        m_ref[h:h + 1, :] = m_new

    def half_step(kb, t_next, s_dst, tmx_dst, vb, s_src, tmx_src):
        bias = mask_bias(t_next)
        for h in range(ATT_HEADS):
            logits_head(kb, h, bias, s_dst, tmx_dst)
            attend_head(vb, h, s_src, tmx_src)

    m_ref[...] = jnp.full(m_ref.shape, NEG_BIG, F32)
    acc_ref[...] = jnp.zeros(acc_ref.shape, F32)

    first = k_copy(0, 1, 1)
    first.start()
    for c in pair_copies(0, 0):
        c.start()
    first.wait()
    bias0 = mask_bias(0)
    for h in range(ATT_HEADS):
        logits_head(kbuf.at[1, 1], h, bias0, sa_ref, tmx_ref.at[0])

    def pair_step(j, carry):
        par = j % 2
        for c in pair_copies(j, par):
            c.wait()

        @pl.when(j + 1 < n2)
        def _():
            for c in pair_copies(j + 1, 1 - par):
                c.start()

        half_step(kbuf.at[par, 0], 2 * j + 1, sb_ref, tmx_ref.at[1], vbuf.at[par, 0], sa_ref, tmx_ref.at[0])
        half_step(kbuf.at[par, 1], jnp.minimum(2 * j + 2, last), sa_ref, tmx_ref.at[0],
                  vbuf.at[par, 1], sb_ref, tmx_ref.at[1])
        return carry

    lax.fori_loop(0, n2, pair_step, 0)

    for h in range(ATT_HEADS):
        a0 = h * V_ROWS
        den = acc_ref[a0 + HEAD_DIM:a0 + HEAD_DIM + 1, :]
        out_ref[h * HEAD_DIM:(h + 1) * HEAD_DIM, :] = acc_ref[a0:a0 + HEAD_DIM, :] / den
    o_ref[...] = out_ref[...].T.astype(BF16)


def _dsa(qT, iqT, iwT, ik, k, vT):
    s = k.shape[0]
    tq = min(DSA_TQ, s)
    tk = min(DSA_TK, s // 2)
    assert s % (2 * tk) == 0 and s % tq == 0
    n_sel = min(TOPK_MAX, s // 4)
    col = lambda i: (0, i)
    return pl.pallas_call(
        functools.partial(_dsa_kernel, n_sel=n_sel),
        grid=(s // tq,),
        in_specs=[
            pl.BlockSpec((512, tq), col),
            pl.BlockSpec((512, tq), col),
            pl.BlockSpec((IDX_HEADS, tq), col),
            pl.BlockSpec((s, LANES), lambda i: (0, 0)),
            pl.BlockSpec(memory_space=pl.ANY),
            pl.BlockSpec(memory_space=pl.ANY),
        ],
        out_specs=pl.BlockSpec((tq, 512), lambda i: (i, 0)),
        out_shape=jax.ShapeDtypeStruct((s, 512), BF16),
        scratch_shapes=[
            pltpu.VMEM((s, tq), jnp.int32),
            pltpu.VMEM((2, 2, tk, 512), BF16),
            pltpu.VMEM((2, 2, ATT_HEADS * V_ROWS, tk), BF16),
            pltpu.SemaphoreType.DMA((2, 2, 2)),
            pltpu.VMEM((IDX_HEADS, LANES, tq), BF16),
            pltpu.VMEM((ATT_HEADS, 256, tq), BF16),
            pltpu.VMEM((ATT_HEADS, tq), F32),
            pltpu.VMEM((2, ATT_HEADS, tq), F32),
            pltpu.VMEM((ATT_HEADS * V_ROWS, tq), F32),
            pltpu.VMEM((512, tq), F32),
            pltpu.VMEM((ATT_HEADS, tk, tq), F32),
            pltpu.VMEM((ATT_HEADS, tk, tq), F32),
        ],
        compiler_params=pltpu.CompilerParams(
            dimension_semantics=("arbitrary",), vmem_limit_bytes=VMEM_LIMIT),
        name="dsa",
    )(qT, iqT, iwT, ik, k, vT)


_C_POOL, _C_MQ, _C_GATE = 0, 512, 1024


def _route(logitsT, bias):
    sc = _sigmoid(logitsT)
    bi = sc + bias
    s_rows = [sc[e:e + 1, :] for e in range(N_EXPERTS)]
    b_rows = [bi[e:e + 1, :] for e in range(N_EXPERTS)]
    best = None
    best_score = None
    for g in range(N_GROUPS):
        v = b_rows[4 * g:4 * g + 4]
        top2 = None
        for a in range(4):
            for c in range(a + 1, 4):
                pair = v[a] + v[c]
                top2 = pair if top2 is None else jnp.maximum(top2, pair)
        if best is None:
            best, best_score = jnp.zeros_like(top2, dtype=jnp.int32), top2
        else:
            upd = top2 > best_score
            best = jnp.where(upd, g, best)
            best_score = jnp.where(upd, top2, best_score)
    wb, ws = [], []
    for m in range(EXPERTS_PER_GROUP):
        vb, vs = b_rows[m], s_rows[m]
        for g in range(1, N_GROUPS):
            vb = jnp.where(best == g, b_rows[4 * g + m], vb)
            vs = jnp.where(best == g, s_rows[4 * g + m], vs)
        wb.append(vb)
        ws.append(vs)
    i1, v1 = jnp.zeros_like(best), wb[0]
    for m in range(1, 4):
        upd = wb[m] > v1
        i1 = jnp.where(upd, m, i1)
        v1 = jnp.where(upd, wb[m], v1)
    i2, v2 = None, None
    for m in range(4):
        cand = jnp.where(i1 == m, -jnp.inf, wb[m])
        if i2 is None:
            i2, v2 = jnp.zeros_like(best), cand
        else:
            upd = cand > v2
            i2 = jnp.where(upd, m, i2)
            v2 = jnp.where(upd, cand, v2)
    s1 = ws[0]
    s2 = ws[0]
    for m in range(1, 4):
        s1 = jnp.where(i1 == m, ws[m], s1)
        s2 = jnp.where(i2 == m, ws[m], s2)
    tot = s1 + s2
    g1, g2 = s1 / tot, s2 / tot
    rows = []
    for e in range(N_EXPERTS):
        g, m = e // 4, e % 4
        val = jnp.where(i1 == m, g1, jnp.where(i2 == m, g2, 0.0))
        rows.append(jnp.where(best == g, val, 0.0))
    return rows


def _mix_kernel(x_ref, xh_ref, dsa_ref, w_ref, b_ref, pw_ref, ps_ref, mk_ref, mv_ref,
                wbr_ref, wout_ref, g_ref, beta_ref, wr_ref, rb_ref,
                x1_ref, gate_ref, uext_ref):
    i = pl.program_id(0)
    tm = x_ref.shape[0]
    x = x_ref[...]
    xb = x.astype(BF16)

    def proj(c0, c1):
        return _dot(xb, w_ref[:, c0:c1]) + b_ref[:, c0:c1]

    u = proj(_C_POOL, _C_MQ)
    uh = _dot(xh_ref[...].astype(BF16), w_ref[:, _C_POOL:_C_MQ]) + b_ref[:, _C_POOL:_C_MQ]
    uext_ref[0:POOL_HALO, :] = jnp.where(i > 0, uh, 0.0)
    uext_ref[POOL_HALO:, :] = u
    t = i * tm + lax.broadcasted_iota(jnp.int32, (tm, 1), 0)
    pooled = []
    for g, w in enumerate(POOL_WINDOWS):
        c0 = g * POOL_GROUP_DIM
        ug = u[:, c0:c0 + POOL_GROUP_DIM]
        win = ug
        for j in range(1, w):
            win = win + uext_ref[POOL_HALO - j:POOL_HALO - j + tm, c0:c0 + POOL_GROUP_DIM]
        cnt = jnp.minimum(t + 1, w).astype(F32)
        d = win / cnt - ug
        pooled.append(_dot(d.astype(BF16), pw_ref[g]))
    pool_out = jnp.concatenate(pooled, axis=-1) * ps_ref[...]

    mq = proj(_C_MQ, _C_GATE)
    mem = []
    for h in range(MEM_HEADS):
        c0 = h * MEM_HEAD_DIM
        s = _dot_nt(mq[:, c0:c0 + MEM_HEAD_DIM].astype(BF16), mk_ref[:, c0:c0 + MEM_HEAD_DIM])
        s = s * (MEM_HEAD_DIM ** -0.5)
        s = s - jnp.max(s, axis=-1, keepdims=True)
        p = jnp.exp(s)
        p = p / jnp.sum(p, axis=-1, keepdims=True)
        mem.append(_dot(p.astype(BF16), mv_ref[:, c0:c0 + MEM_HEAD_DIM]))
    mem_out = jnp.concatenate(mem, axis=-1)

    merged = jnp.zeros((tm, D_MODEL), F32)
    for n, br in enumerate((pool_out.astype(BF16), dsa_ref[...], mem_out.astype(BF16))):
        gate = _sigmoid(proj(_C_GATE + n * D_MODEL, _C_GATE + (n + 1) * D_MODEL))
        merged = merged + gate * _dot(br, wbr_ref[n])
    mix = _dot(merged.astype(BF16), wout_ref[...])
    x1 = _layer_norm(DN_ALPHA * x + mix, g_ref[...], beta_ref[...])
    x1_ref[...] = x1

    logitsT = _dot_nt(wr_ref[...], x1.astype(BF16))
    rows = _route(logitsT, rb_ref[...])
    gT = jnp.concatenate(rows + [jnp.zeros((LANES - N_EXPERTS, tm), F32)], axis=0)
    gate_ref[...] = gT.T


def _mix(x2, dsa, w, b, pw, ps, mk, mv, wbr, wout, g, beta, wr, rb):
    s = x2.shape[0]
    tm = min(MIX_TM, s)
    const2 = lambda i: (0, 0)
    const3 = lambda i: (0, 0, 0)
    hb = tm // POOL_HALO
    return pl.pallas_call(
        _mix_kernel,
        grid=(s // tm,),
        in_specs=[
            pl.BlockSpec((tm, D_MODEL), lambda i: (i, 0)),
            pl.BlockSpec((POOL_HALO, D_MODEL), lambda i: (jnp.maximum(i * hb - 1, 0), 0)),
            pl.BlockSpec((tm, BRANCH_WIDTH), lambda i: (i, 0)),
            pl.BlockSpec(w.shape, const2),
            pl.BlockSpec(b.shape, const2),
            pl.BlockSpec(pw.shape, const3),
            pl.BlockSpec(ps.shape, const2),
            pl.BlockSpec(mk.shape, const2),
            pl.BlockSpec(mv.shape, const2),
            pl.BlockSpec(wbr.shape, const3),
            pl.BlockSpec(wout.shape, const2),
            pl.BlockSpec(g.shape, const2),
            pl.BlockSpec(beta.shape, const2),
            pl.BlockSpec(wr.shape, const2),
            pl.BlockSpec(rb.shape, const2),
        ],
        out_specs=[
            pl.BlockSpec((tm, D_MODEL), lambda i: (i, 0)),
            pl.BlockSpec((tm, LANES), lambda i: (i, 0)),
        ],
        out_shape=[
            jax.ShapeDtypeStruct((s, D_MODEL), F32),
            jax.ShapeDtypeStruct((s, LANES), F32),
        ],
        scratch_shapes=[pltpu.VMEM((POOL_HALO + tm, BRANCH_WIDTH), F32)],
        compiler_params=pltpu.CompilerParams(
            dimension_semantics=("parallel",), vmem_limit_bytes=VMEM_LIMIT),
        name="mix",
    )(x2, x2, dsa, w, b, pw, ps, mk, mv, wbr, wout, g, beta, wr, rb)


def _moe_kernel(x_ref, gate_ref, w1_ref, w3_ref, w2_ref, g_ref, beta_ref, o_ref, xb_ref, acc_ref):
    e = pl.program_id(1)

    @pl.when(e == 0)
    def _():
        xb_ref[...] = x_ref[...].astype(BF16)
        acc_ref[...] = jnp.zeros(acc_ref.shape, F32)

    xb = xb_ref[...]
    h1 = _dot(xb, w1_ref[0])
    h3 = _dot(xb, w3_ref[0])
    h = (h1 * _sigmoid(h1)) * h3
    y = _dot(h.astype(BF16), w2_ref[0])
    lane = lax.broadcasted_iota(jnp.int32, gate_ref.shape, 1)
    ge = jnp.sum(jnp.where(lane == e, gate_ref[...], 0.0), axis=-1, keepdims=True)
    acc_ref[...] += y * ge

    @pl.when(e == N_EXPERTS - 1)
    def _():
        o_ref[...] = _layer_norm(DN_ALPHA * x_ref[...] + acc_ref[...], g_ref[...], beta_ref[...])


def _moe(x1, gates, w1, w3, w2, g, beta):
    s = x1.shape[0]
    tm = min(MOE_TM, s)
    const2 = lambda i, e: (0, 0)
    return pl.pallas_call(
        _moe_kernel,
        grid=(s // tm, N_EXPERTS),
        in_specs=[
            pl.BlockSpec((tm, D_MODEL), lambda i, e: (i, 0)),
            pl.BlockSpec((tm, LANES), lambda i, e: (i, 0)),
            pl.BlockSpec((1, D_MODEL, D_EXPERT), lambda i, e: (e, 0, 0)),
            pl.BlockSpec((1, D_MODEL, D_EXPERT), lambda i, e: (e, 0, 0)),
            pl.BlockSpec((1, D_EXPERT, D_MODEL), lambda i, e: (e, 0, 0)),
            pl.BlockSpec(g.shape, const2),
            pl.BlockSpec(beta.shape, const2),
        ],
        out_specs=pl.BlockSpec((tm, D_MODEL), lambda i, e: (i, 0)),
        out_shape=jax.ShapeDtypeStruct((s, D_MODEL), F32),
        scratch_shapes=[pltpu.VMEM((tm, D_MODEL), BF16), pltpu.VMEM((tm, D_MODEL), F32)],
        compiler_params=pltpu.CompilerParams(
            dimension_semantics=("parallel", "arbitrary"), vmem_limit_bytes=VMEM_LIMIT),
        name="moe",
    )(x1, gates, w1, w3, w2, g, beta)


def _split_w_in(w_in, b_in):
    o = [0, 512, 1024, 1536, 2048, 2560, 2624, 2632, 3144, 6216]
    pool, q, k, v, iq, ik, iw, mq, gates = [(w_in[:, a:c], b_in[a:c]) for a, c in zip(o[:-1], o[1:])]
    qs = HEAD_DIM ** -0.5 * LOG2_E
    iws = (HEAD_DIM ** -0.5) * (IDX_HEADS ** -0.5)
    pad = _R_END - _R_IW - IDX_HEADS
    wT = jnp.concatenate([q[0] * qs, iq[0], k[0], ik[0], v[0], iw[0] * iws,
                          jnp.zeros((D_MODEL, pad), F32)], axis=1).T
    bT = jnp.concatenate([q[1] * qs, iq[1], k[1], ik[1], v[1], iw[1] * iws, jnp.zeros((pad,), F32)])
    w_mix = jnp.concatenate([pool[0], mq[0], gates[0]], axis=1)
    b_mix = jnp.concatenate([pool[1], mq[1], gates[1]])
    return wT.astype(BF16), bT[:, None], w_mix.astype(BF16), b_mix[None, :]


def kernel(x, mem, positions, w_in, b_in, pool_w, pool_scale, w_mem_kv, w_br, w_out, ln1_g, ln1_b,
           w_router, router_bias, w1, w3, w2, ln2_g, ln2_b):
    B, S, D = x.shape
    inv_freq = ROPE_THETA ** (-jnp.arange(ROPE_HALF, dtype=F32) / ROPE_HALF)
    wr = w_router.T.astype(BF16)
    rb = router_bias.astype(F32)[:, None]
    outs = []
    for bi in range(B):
        xs = x[bi]
        ang = positions[bi].astype(F32)[None, :] * inv_freq[:, None]
        cosT, sinT = jnp.cos(ang), jnp.sin(ang)
        for l in range(DEPTH):
            wT, bT, w_mix, b_mix = _split_w_in(w_in[l], b_in[l])
            kv = _matmul(mem[bi], w_mem_kv[l].astype(BF16))
            mk = kv[:, :BRANCH_WIDTH].astype(BF16)
            mv = kv[:, BRANCH_WIDTH:].astype(BF16)
            qT, iqT, k, ik, vT, iwT = _proj_rope(xs, wT, bT, cosT, sinT)
            dsa = _dsa(qT, iqT, iwT, ik, k, vT)
            x1, gates = _mix(xs, dsa, w_mix, b_mix, pool_w[l].astype(BF16), pool_scale[l][None, :],
                             mk, mv, w_br[l].astype(BF16), w_out[l].astype(BF16),
                             ln1_g[l][None, :], ln1_b[l][None, :], wr, rb)
            xs = _moe(x1, gates, w1[l].astype(BF16), w3[l].astype(BF16), w2[l].astype(BF16),
                      ln2_g[l][None, :], ln2_b[l][None, :])
        outs.append(xs)
    return jnp.stack(outs, axis=0)
```

```python
import functools

import jax
import jax.numpy as jnp
from jax import lax
from jax.experimental import pallas as pl
from jax.experimental.pallas import tpu as pltpu

D_MODEL = 1024
DEPTH = 2
CHUNK = 64
ROPE_THETA = 10000.0
LN_EPS = 1e-5
POOL_WINDOWS = (2, 4, 8, 16)
POOL_GROUP_DIM = 128
POOL_HALO = 16
ATT_HEADS = 8
HEAD_DIM = 64
ROPE_HALF = HEAD_DIM // 2
V_ROWS = HEAD_DIM + 16
LOG2_E = 1.4426950408889634
IDX_HEADS = 8
TOPK_MAX = 256
MEM_HEADS = 4
MEM_HEAD_DIM = 128
BRANCH_WIDTH = 512
N_BRANCH = 3
N_EXPERTS = 16
N_GROUPS = 4
EXPERTS_PER_GROUP = 4
D_EXPERT = 512
DN_ALPHA = (2 * DEPTH) ** 0.25

LANES = 128
INT_MIN = -(2 ** 31)
HALF_OFFSET = 2 ** 15
NEG_BIG = -1e30

PROJ_TM = 512
DSA_TQ = 256
DSA_TK = 512
MIX_TM = 512
MOE_TM = 1024
VMEM_LIMIT = 56 * 1024 * 1024

F32 = jnp.float32
BF16 = jnp.bfloat16


def _dot(a, b):
    return jnp.dot(a, b, preferred_element_type=F32)


def _dot_nt(a, b):
    return lax.dot_general(a, b, (((1,), (1,)), ((), ())), preferred_element_type=F32)


def _layer_norm(y, g, b):
    mu = jnp.mean(y, axis=-1, keepdims=True)
    yc = y - mu
    var = jnp.mean(yc * yc, axis=-1, keepdims=True)
    return yc * lax.rsqrt(var + LN_EPS) * g + b


def _sigmoid(x):
    return 1.0 / (1.0 + jnp.exp(-x))


def _matmul_kernel(a_ref, b_ref, o_ref):
    o_ref[...] = _dot(a_ref[...].astype(BF16), b_ref[...])


def _matmul(a, b):
    m, n = a.shape[0], b.shape[1]
    return pl.pallas_call(
        _matmul_kernel,
        out_shape=jax.ShapeDtypeStruct((m, n), F32),
        name="mem_kv",
    )(a, b)


_R_Q, _R_IQ, _R_K, _R_IK, _R_V, _R_IW, _R_END = 0, 512, 1024, 1536, 1600, 2112, 2128


def _proj_rope_kernel(x_ref, w_ref, b_ref, cos_ref, sin_ref,
                      qT_ref, iqT_ref, k_ref, ik_ref, vT_ref, iwT_ref, kT_scr, ikT_scr):
    xb = x_ref[...].astype(BF16)
    cos = cos_ref[...]
    sin = sin_ref[...]

    def proj(r0, r1):
        return _dot_nt(w_ref[r0:r1, :], xb) + b_ref[r0:r1, :]

    def rope_store(r, dst_ref, dtype):
        for h in range(r.shape[0] // HEAD_DIM):
            a = h * HEAD_DIM
            x1 = r[a:a + ROPE_HALF]
            x2 = r[a + ROPE_HALF:a + HEAD_DIM]
            dst_ref[a:a + ROPE_HALF, :] = (x1 * cos - x2 * sin).astype(dtype)
            dst_ref[a + ROPE_HALF:a + HEAD_DIM, :] = (x2 * cos + x1 * sin).astype(dtype)

    rope_store(proj(_R_Q, _R_IQ), qT_ref, BF16)
    rope_store(proj(_R_IQ, _R_K), iqT_ref, BF16)
    rope_store(proj(_R_K, _R_IK), kT_scr, F32)
    k_ref[...] = kT_scr[...].T.astype(BF16)
    ikT_scr[HEAD_DIM:, :] = jnp.zeros((LANES - HEAD_DIM, ikT_scr.shape[1]), F32)
    rope_store(proj(_R_IK, _R_V), ikT_scr, F32)
    ik_ref[...] = ikT_scr[...].T.astype(BF16)
    v = proj(_R_V, _R_IW).astype(BF16)
    tm = v.shape[1]
    pad_row = lax.broadcasted_iota(jnp.int32, (V_ROWS - HEAD_DIM, tm), 0)
    ones_then_zeros = jnp.where(pad_row == 0, 1.0, 0.0).astype(BF16)
    for h in range(ATT_HEADS):
        a0 = h * V_ROWS
        vT_ref[a0:a0 + HEAD_DIM, :] = v[h * HEAD_DIM:(h + 1) * HEAD_DIM]
        vT_ref[a0 + HEAD_DIM:a0 + V_ROWS, :] = ones_then_zeros
    iwT_ref[...] = proj(_R_IW, _R_END)[0:IDX_HEADS]


def _proj_rope(x2, wT, bT, cosT, sinT):
    s = x2.shape[0]
    tm = min(PROJ_TM, s)
    col = lambda i: (0, i)
    return pl.pallas_call(
        _proj_rope_kernel,
        grid=(s // tm,),
        in_specs=[
            pl.BlockSpec((tm, D_MODEL), lambda i: (i, 0)),
            pl.BlockSpec((_R_END, D_MODEL), lambda i: (0, 0)),
            pl.BlockSpec((_R_END, 1), lambda i: (0, 0)),
            pl.BlockSpec((ROPE_HALF, tm), col),
            pl.BlockSpec((ROPE_HALF, tm), col),
        ],
        out_specs=[
            pl.BlockSpec((512, tm), col),
            pl.BlockSpec((512, tm), col),
            pl.BlockSpec((tm, 512), lambda i: (i, 0)),
            pl.BlockSpec((tm, LANES), lambda i: (i, 0)),
            pl.BlockSpec((ATT_HEADS * V_ROWS, tm), col),
            pl.BlockSpec((IDX_HEADS, tm), col),
        ],
        out_shape=[
            jax.ShapeDtypeStruct((512, s), BF16),
            jax.ShapeDtypeStruct((512, s), BF16),
            jax.ShapeDtypeStruct((s, 512), BF16),
            jax.ShapeDtypeStruct((s, LANES), BF16),
            jax.ShapeDtypeStruct((ATT_HEADS * V_ROWS, s), BF16),
            jax.ShapeDtypeStruct((IDX_HEADS, s), F32),
        ],
        scratch_shapes=[pltpu.VMEM((512, tm), F32), pltpu.VMEM((LANES, tm), F32)],
        compiler_params=pltpu.CompilerParams(
            dimension_semantics=("parallel",), vmem_limit_bytes=VMEM_LIMIT),
        name="proj_rope",
    )(x2, wT, bT, cosT, sinT)


def _dsa_kernel(qT_ref, iqT_ref, iwT_ref, ik_ref, k_hbm, vT_hbm, o_ref,
                hi_ref, lo_ref, kbuf, vbuf, sem, iqp_ref, qp_ref, m_ref, tmx_ref, acc_ref, out_ref,
                sa_ref, sb_ref, *, n_sel):
    tq = qT_ref.shape[1]
    tk = kbuf.shape[2]
    i = pl.program_id(0)
    q0 = i * tq
    n_kt = (q0 + tq + tk - 1) // tk

    zpad = jnp.zeros((LANES - HEAD_DIM, tq), BF16)
    for h in range(IDX_HEADS):
        iqp_ref[h, 0:HEAD_DIM, :] = iqT_ref[h * HEAD_DIM:(h + 1) * HEAD_DIM, :]
        iqp_ref[h, HEAD_DIM:, :] = zpad
    row = lax.broadcasted_iota(jnp.int32, (256, tq), 0)
    for h in range(ATT_HEADS):
        j, hh = h // 4, h % 4
        quad = qT_ref[256 * j:256 * (j + 1), :]
        keep = (row >= hh * HEAD_DIM) & (row < (hh + 1) * HEAD_DIM)
        qp_ref[h] = jnp.where(keep, quad, jnp.zeros_like(quad))

    q_chunk = (q0 + lax.broadcasted_iota(jnp.int32, (1, tq), 1)) // CHUNK

    def score_tile(kt, carry):
        ks = pl.multiple_of(kt * tk, tk)
        ikt = ik_ref[pl.ds(ks, tk), :]
        acc = jnp.zeros((tk, tq), F32)
        for h in range(IDX_HEADS):
            z = _dot(ikt, iqp_ref[h])
            acc = acc + jnp.maximum(z, 0.0) * iwT_ref[h:h + 1, :]
        acc = acc + 0.0
        bits = pltpu.bitcast(acc, jnp.int32)
        skey = bits ^ ((bits >> 31) & jnp.int32(0x7FFFFFFF))
        key_chunk = (ks + lax.broadcasted_iota(jnp.int32, (tk, 1), 0)) // CHUNK
        skey = jnp.where(key_chunk <= q_chunk, skey, jnp.int32(INT_MIN))
        hi_ref[pl.ds(ks, tk), :] = (skey >> 16).astype(jnp.int16)
        lo_ref[pl.ds(ks, tk), :] = ((skey & jnp.int32(0xFFFF)) - HALF_OFFSET).astype(jnp.int16)
        return carry

    lax.fori_loop(0, n_kt, score_tile, 0)

    one16, zero16 = jnp.int16(1), jnp.int16(0)

    def add_rows(c, flags):
        parts = [flags[16 * r:16 * (r + 1), :] for r in range(tk // 16)]
        while len(parts) > 1:
            parts = [a + b for a, b in zip(parts[0::2], parts[1::2])]
        return c + parts[0]

    def total(c):
        return jnp.sum(c.astype(jnp.int32), axis=0, keepdims=True)

    def select16(plane_ref, need):
        def step(b, ans_u):
            cand_u = ans_u | (jnp.int32(1) << (15 - b))
            cand = (cand_u - HALF_OFFSET).astype(jnp.int16)

            def body(kt, c):
                ks = pl.multiple_of(kt * tk, tk)
                return add_rows(c, jnp.where(plane_ref[pl.ds(ks, tk), :] >= cand, one16, zero16))

            cnt = total(lax.fori_loop(0, n_kt, body, jnp.zeros((16, tq), jnp.int16)))
            return jnp.where(cnt >= need, cand_u, ans_u)

        return lax.fori_loop(0, 16, step, jnp.zeros((1, tq), jnp.int32))

    hi_u = select16(hi_ref, n_sel)
    hi_t = (hi_u - HALF_OFFSET).astype(jnp.int16)

    def restrict_low(kt, c):
        ks = pl.multiple_of(kt * tk, tk)
        hi = hi_ref[pl.ds(ks, tk), :]
        lo_ref[pl.ds(ks, tk), :] = jnp.where(hi == hi_t, lo_ref[pl.ds(ks, tk), :], jnp.int16(-HALF_OFFSET))
        return add_rows(c, jnp.where(hi > hi_t, one16, zero16))

    n_above = total(lax.fori_loop(0, n_kt, restrict_low, jnp.zeros((16, tq), jnp.int16)))
    lo_u = select16(lo_ref, n_sel - n_above)
    lo_u = jnp.where((hi_u == 0) & (lo_u == 0), 1, lo_u)
    lo_t = (lo_u - HALF_OFFSET).astype(jnp.int16)

    n2 = (n_kt + 1) // 2
    last = 2 * n2 - 1

    @pl.when(n_kt % 2 == 1)
    def _():
        sentinel = jnp.full((tk, tq), -HALF_OFFSET, jnp.int16)
        hi_ref[pl.ds(pl.multiple_of(n_kt * tk, tk), tk), :] = sentinel
        lo_ref[pl.ds(pl.multiple_of(n_kt * tk, tk), tk), :] = sentinel

    def k_copy(t, par, w):
        return pltpu.make_async_copy(k_hbm.at[pl.ds(t * tk, tk), :], kbuf.at[par, w], sem.at[0, par, w])

    def v_copy(t, par, w):
        return pltpu.make_async_copy(vT_hbm.at[:, pl.ds(t * tk, tk)], vbuf.at[par, w], sem.at[1, par, w])

    def pair_copies(j, par):
        return (k_copy(2 * j + 1, par, 0), k_copy(jnp.minimum(2 * j + 2, last), par, 1),
                v_copy(2 * j, par, 0), v_copy(2 * j + 1, par, 1))

    def logits_head(kb, h, bias, s_dst, tmx_dst):
        j = h // 4
        s = _dot(kb[:, 256 * j:256 * (j + 1)], qp_ref[h]) + bias
        s_dst[h] = s
        tmx_dst[h:h + 1, :] = jnp.max(s, axis=0, keepdims=True)

    def mask_bias(t):
        ks = pl.multiple_of(t * tk, tk)
        hi = hi_ref[pl.ds(ks, tk), :]
        sel = (hi > hi_t) | ((hi == hi_t) & (lo_ref[pl.ds(ks, tk), :] >= lo_t))
        return jnp.where(sel, jnp.zeros((), BF16), jnp.full((), NEG_BIG, BF16)).astype(F32)

    def attend_head(vb, h, s_src, tmx_src):
        m_old = m_ref[h:h + 1, :]
        m_new = jnp.maximum(m_old, tmx_src[h:h + 1, :])
        alpha = jnp.exp2(m_old - m_new)
        p = jnp.exp2(s_src[h] - m_new)
        a0 = h * V_ROWS
        pv = _dot(vb[a0:a0 + V_ROWS, :], p.astype(BF16))
        acc_ref[a0:a0 + V_ROWS, :] = alpha * acc_ref[a0:a0 + V_ROWS, :] + pv
        m_ref[h:h + 1, :] = m_new

    def half_step(kb, t_next, s_dst, tmx_dst, vb, s_src, tmx_src):
        bias = mask_bias(t_next)
        for h in range(ATT_HEADS):
            logits_head(kb, h, bias, s_dst, tmx_dst)
            attend_head(vb, h, s_src, tmx_src)

    m_ref[...] = jnp.full(m_ref.shape, NEG_BIG, F32)
    acc_ref[...] = jnp.zeros(acc_ref.shape, F32)

    first = k_copy(0, 1, 1)
    first.start()
    for c in pair_copies(0, 0):
        c.start()
    first.wait()
    bias0 = mask_bias(0)
    for h in range(ATT_HEADS):
        logits_head(kbuf.at[1, 1], h, bias0, sa_ref, tmx_ref.at[0])

    def pair_step(j, carry):
        par = j % 2
        for c in pair_copies(j, par):
            c.wait()

        @pl.when(j + 1 < n2)
        def _():
            for c in pair_copies(j + 1, 1 - par):
                c.start()

        half_step(kbuf.at[par, 0], 2 * j + 1, sb_ref, tmx_ref.at[1], vbuf.at[par, 0], sa_ref, tmx_ref.at[0])
        half_step(kbuf.at[par, 1], jnp.minimum(2 * j + 2, last), sa_ref, tmx_ref.at[0],
                  vbuf.at[par, 1], sb_ref, tmx_ref.at[1])
        return carry

    lax.fori_loop(0, n2, pair_step, 0)

    for h in range(ATT_HEADS):
        a0 = h * V_ROWS
        den = acc_ref[a0 + HEAD_DIM:a0 + HEAD_DIM + 1, :]
        out_ref[h * HEAD_DIM:(h + 1) * HEAD_DIM, :] = acc_ref[a0:a0 + HEAD_DIM, :] / den
    o_ref[...] = out_ref[...].T.astype(BF16)


def _dsa(qT, iqT, iwT, ik, k, vT):
    s = k.shape[0]
    tq = min(DSA_TQ, s)
    tk = min(DSA_TK, s // 2)
    assert s % (2 * tk) == 0 and s % tq == 0
    n_sel = min(TOPK_MAX, s // 4)
    col = lambda i: (0, i)
    return pl.pallas_call(
        functools.partial(_dsa_kernel, n_sel=n_sel),
        grid=(s // tq,),
        in_specs=[
            pl.BlockSpec((512, tq), col),
            pl.BlockSpec((512, tq), col),
            pl.BlockSpec((IDX_HEADS, tq), col),
            pl.BlockSpec((s, LANES), lambda i: (0, 0)),
            pl.BlockSpec(memory_space=pl.ANY),
            pl.BlockSpec(memory_space=pl.ANY),
        ],
        out_specs=pl.BlockSpec((tq, 512), lambda i: (i, 0)),
        out_shape=jax.ShapeDtypeStruct((s, 512), BF16),
        scratch_shapes=[
            pltpu.VMEM((s, tq), jnp.int16),
            pltpu.VMEM((s, tq), jnp.int16),
            pltpu.VMEM((2, 2, tk, 512), BF16),
            pltpu.VMEM((2, 2, ATT_HEADS * V_ROWS, tk), BF16),
            pltpu.SemaphoreType.DMA((2, 2, 2)),
            pltpu.VMEM((IDX_HEADS, LANES, tq), BF16),
            pltpu.VMEM((ATT_HEADS, 256, tq), BF16),
            pltpu.VMEM((ATT_HEADS, tq), F32),
            pltpu.VMEM((2, ATT_HEADS, tq), F32),
            pltpu.VMEM((ATT_HEADS * V_ROWS, tq), F32),
            pltpu.VMEM((512, tq), F32),
            pltpu.VMEM((ATT_HEADS, tk, tq), F32),
            pltpu.VMEM((ATT_HEADS, tk, tq), F32),
        ],
        compiler_params=pltpu.CompilerParams(
            dimension_semantics=("arbitrary",), vmem_limit_bytes=VMEM_LIMIT),
        name="dsa",
    )(qT, iqT, iwT, ik, k, vT)


_C_POOL, _C_MQ, _C_GATE = 0, 512, 1024


def _route(logitsT, bias):
    sc = _sigmoid(logitsT)
    bi = sc + bias
    s_rows = [sc[e:e + 1, :] for e in range(N_EXPERTS)]
    b_rows = [bi[e:e + 1, :] for e in range(N_EXPERTS)]
    best = None
    best_score = None
    for g in range(N_GROUPS):
        v = b_rows[4 * g:4 * g + 4]
        top2 = None
        for a in range(4):
            for c in range(a + 1, 4):
                pair = v[a] + v[c]
                top2 = pair if top2 is None else jnp.maximum(top2, pair)
        if best is None:
            best, best_score = jnp.zeros_like(top2, dtype=jnp.int32), top2
        else:
            upd = top2 > best_score
            best = jnp.where(upd, g, best)
            best_score = jnp.where(upd, top2, best_score)
    wb, ws = [], []
    for m in range(EXPERTS_PER_GROUP):
        vb, vs = b_rows[m], s_rows[m]
        for g in range(1, N_GROUPS):
            vb = jnp.where(best == g, b_rows[4 * g + m], vb)
            vs = jnp.where(best == g, s_rows[4 * g + m], vs)
        wb.append(vb)
        ws.append(vs)
    i1, v1 = jnp.zeros_like(best), wb[0]
    for m in range(1, 4):
        upd = wb[m] > v1
        i1 = jnp.where(upd, m, i1)
        v1 = jnp.where(upd, wb[m], v1)
    i2, v2 = None, None
    for m in range(4):
        cand = jnp.where(i1 == m, -jnp.inf, wb[m])
        if i2 is None:
            i2, v2 = jnp.zeros_like(best), cand
        else:
            upd = cand > v2
            i2 = jnp.where(upd, m, i2)
            v2 = jnp.where(upd, cand, v2)
    s1 = ws[0]
    s2 = ws[0]
    for m in range(1, 4):
        s1 = jnp.where(i1 == m, ws[m], s1)
        s2 = jnp.where(i2 == m, ws[m], s2)
    tot = s1 + s2
    g1, g2 = s1 / tot, s2 / tot
    rows = []
    for e in range(N_EXPERTS):
        g, m = e // 4, e % 4
        val = jnp.where(i1 == m, g1, jnp.where(i2 == m, g2, 0.0))
        rows.append(jnp.where(best == g, val, 0.0))
    return rows


def _mix_kernel(x_ref, xh_ref, dsa_ref, w_ref, b_ref, pw_ref, ps_ref, mk_ref, mv_ref,
                wbr_ref, wout_ref, g_ref, beta_ref, wr_ref, rb_ref,
                x1_ref, gate_ref, uext_ref):
    i = pl.program_id(0)
    tm = x_ref.shape[0]
    x = x_ref[...]
    xb = x.astype(BF16)

    def proj(c0, c1):
        return _dot(xb, w_ref[:, c0:c1]) + b_ref[:, c0:c1]

    u = proj(_C_POOL, _C_MQ)
    uh = _dot(xh_ref[...].astype(BF16), w_ref[:, _C_POOL:_C_MQ]) + b_ref[:, _C_POOL:_C_MQ]
    uext_ref[0:POOL_HALO, :] = jnp.where(i > 0, uh, 0.0)
    uext_ref[POOL_HALO:, :] = u
    t = i * tm + lax.broadcasted_iota(jnp.int32, (tm, 1), 0)
    pooled = []
    for g, w in enumerate(POOL_WINDOWS):
        c0 = g * POOL_GROUP_DIM
        ug = u[:, c0:c0 + POOL_GROUP_DIM]
        win = ug
        for j in range(1, w):
            win = win + uext_ref[POOL_HALO - j:POOL_HALO - j + tm, c0:c0 + POOL_GROUP_DIM]
        cnt = jnp.minimum(t + 1, w).astype(F32)
        d = win / cnt - ug
        pooled.append(_dot(d.astype(BF16), pw_ref[g]))
    pool_out = jnp.concatenate(pooled, axis=-1) * ps_ref[...]

    mq = proj(_C_MQ, _C_GATE)
    mem = []
    for h in range(MEM_HEADS):
        c0 = h * MEM_HEAD_DIM
        s = _dot_nt(mq[:, c0:c0 + MEM_HEAD_DIM].astype(BF16), mk_ref[:, c0:c0 + MEM_HEAD_DIM])
        s = s * (MEM_HEAD_DIM ** -0.5)
        s = s - jnp.max(s, axis=-1, keepdims=True)
        p = jnp.exp(s)
        p = p / jnp.sum(p, axis=-1, keepdims=True)
        mem.append(_dot(p.astype(BF16), mv_ref[:, c0:c0 + MEM_HEAD_DIM]))
    mem_out = jnp.concatenate(mem, axis=-1)

    merged = jnp.zeros((tm, D_MODEL), F32)
    for n, br in enumerate((pool_out.astype(BF16), dsa_ref[...], mem_out.astype(BF16))):
        gate = _sigmoid(proj(_C_GATE + n * D_MODEL, _C_GATE + (n + 1) * D_MODEL))
        merged = merged + gate * _dot(br, wbr_ref[n])
    mix = _dot(merged.astype(BF16), wout_ref[...])
    x1 = _layer_norm(DN_ALPHA * x + mix, g_ref[...], beta_ref[...])
    x1_ref[...] = x1

    logitsT = _dot_nt(wr_ref[...], x1.astype(BF16))
    rows = _route(logitsT, rb_ref[...])
    gT = jnp.concatenate(rows + [jnp.zeros((LANES - N_EXPERTS, tm), F32)], axis=0)
    gate_ref[...] = gT.T


def _mix(x2, dsa, w, b, pw, ps, mk, mv, wbr, wout, g, beta, wr, rb):
    s = x2.shape[0]
    tm = min(MIX_TM, s)
    const2 = lambda i: (0, 0)
    const3 = lambda i: (0, 0, 0)
    hb = tm // POOL_HALO
    return pl.pallas_call(
        _mix_kernel,
        grid=(s // tm,),
        in_specs=[
            pl.BlockSpec((tm, D_MODEL), lambda i: (i, 0)),
            pl.BlockSpec((POOL_HALO, D_MODEL), lambda i: (jnp.maximum(i * hb - 1, 0), 0)),
            pl.BlockSpec((tm, BRANCH_WIDTH), lambda i: (i, 0)),
            pl.BlockSpec(w.shape, const2),
            pl.BlockSpec(b.shape, const2),
            pl.BlockSpec(pw.shape, const3),
            pl.BlockSpec(ps.shape, const2),
            pl.BlockSpec(mk.shape, const2),
            pl.BlockSpec(mv.shape, const2),
            pl.BlockSpec(wbr.shape, const3),
            pl.BlockSpec(wout.shape, const2),
            pl.BlockSpec(g.shape, const2),
            pl.BlockSpec(beta.shape, const2),
            pl.BlockSpec(wr.shape, const2),
            pl.BlockSpec(rb.shape, const2),
        ],
        out_specs=[
            pl.BlockSpec((tm, D_MODEL), lambda i: (i, 0)),
            pl.BlockSpec((tm, LANES), lambda i: (i, 0)),
        ],
        out_shape=[
            jax.ShapeDtypeStruct((s, D_MODEL), F32),
            jax.ShapeDtypeStruct((s, LANES), F32),
        ],
        scratch_shapes=[pltpu.VMEM((POOL_HALO + tm, BRANCH_WIDTH), F32)],
        compiler_params=pltpu.CompilerParams(
            dimension_semantics=("parallel",), vmem_limit_bytes=VMEM_LIMIT),
        name="mix",
    )(x2, x2, dsa, w, b, pw, ps, mk, mv, wbr, wout, g, beta, wr, rb)


def _moe_kernel(x_ref, gate_ref, w1_ref, w3_ref, w2_ref, g_ref, beta_ref, o_ref, xb_ref, acc_ref):
    e = pl.program_id(1)

    @pl.when(e == 0)
    def _():
        xb_ref[...] = x_ref[...].astype(BF16)
        acc_ref[...] = jnp.zeros(acc_ref.shape, F32)

    xb = xb_ref[...]
    h1 = _dot(xb, w1_ref[0])
    h3 = _dot(xb, w3_ref[0])
    h = (h1 * _sigmoid(h1)) * h3
    y = _dot(h.astype(BF16), w2_ref[0])
    lane = lax.broadcasted_iota(jnp.int32, gate_ref.shape, 1)
    ge = jnp.sum(jnp.where(lane == e, gate_ref[...], 0.0), axis=-1, keepdims=True)
    acc_ref[...] += y * ge

    @pl.when(e == N_EXPERTS - 1)
    def _():
        o_ref[...] = _layer_norm(DN_ALPHA * x_ref[...] + acc_ref[...], g_ref[...], beta_ref[...])


def _moe(x1, gates, w1, w3, w2, g, beta):
    s = x1.shape[0]
    tm = min(MOE_TM, s)
    const2 = lambda i, e: (0, 0)
    return pl.pallas_call(
        _moe_kernel,
        grid=(s // tm, N_EXPERTS),
        in_specs=[
            pl.BlockSpec((tm, D_MODEL), lambda i, e: (i, 0)),
            pl.BlockSpec((tm, LANES), lambda i, e: (i, 0)),
            pl.BlockSpec((1, D_MODEL, D_EXPERT), lambda i, e: (e, 0, 0)),
            pl.BlockSpec((1, D_MODEL, D_EXPERT), lambda i, e: (e, 0, 0)),
            pl.BlockSpec((1, D_EXPERT, D_MODEL), lambda i, e: (e, 0, 0)),
            pl.BlockSpec(g.shape, const2),
            pl.BlockSpec(beta.shape, const2),
        ],
        out_specs=pl.BlockSpec((tm, D_MODEL), lambda i, e: (i, 0)),
        out_shape=jax.ShapeDtypeStruct((s, D_MODEL), F32),
        scratch_shapes=[pltpu.VMEM((tm, D_MODEL), BF16), pltpu.VMEM((tm, D_MODEL), F32)],
        compiler_params=pltpu.CompilerParams(
            dimension_semantics=("parallel", "arbitrary"), vmem_limit_bytes=VMEM_LIMIT),
        name="moe",
    )(x1, gates, w1, w3, w2, g, beta)


def _split_w_in(w_in, b_in):
    o = [0, 512, 1024, 1536, 2048, 2560, 2624, 2632, 3144, 6216]
    pool, q, k, v, iq, ik, iw, mq, gates = [(w_in[:, a:c], b_in[a:c]) for a, c in zip(o[:-1], o[1:])]
    qs = HEAD_DIM ** -0.5 * LOG2_E
    iws = (HEAD_DIM ** -0.5) * (IDX_HEADS ** -0.5)
    pad = _R_END - _R_IW - IDX_HEADS
    wT = jnp.concatenate([q[0] * qs, iq[0], k[0], ik[0], v[0], iw[0] * iws,
                          jnp.zeros((D_MODEL, pad), F32)], axis=1).T
    bT = jnp.concatenate([q[1] * qs, iq[1], k[1], ik[1], v[1], iw[1] * iws, jnp.zeros((pad,), F32)])
    w_mix = jnp.concatenate([pool[0], mq[0], gates[0]], axis=1)
    b_mix = jnp.concatenate([pool[1], mq[1], gates[1]])
    return wT.astype(BF16), bT[:, None], w_mix.astype(BF16), b_mix[None, :]


def kernel(x, mem, positions, w_in, b_in, pool_w, pool_scale, w_mem_kv, w_br, w_out, ln1_g, ln1_b,
           w_router, router_bias, w1, w3, w2, ln2_g, ln2_b):
    B, S, D = x.shape
    inv_freq = ROPE_THETA ** (-jnp.arange(ROPE_HALF, dtype=F32) / ROPE_HALF)
    wr = w_router.T.astype(BF16)
    rb = router_bias.astype(F32)[:, None]
    outs = []
    for bi in range(B):
        xs = x[bi]
        ang = positions[bi].astype(F32)[None, :] * inv_freq[:, None]
        cosT, sinT = jnp.cos(ang), jnp.sin(ang)
        for l in range(DEPTH):
            wT, bT, w_mix, b_mix = _split_w_in(w_in[l], b_in[l])
            kv = _matmul(mem[bi], w_mem_kv[l].astype(BF16))
            mk = kv[:, :BRANCH_WIDTH].astype(BF16)
            mv = kv[:, BRANCH_WIDTH:].astype(BF16)
            qT, iqT, k, ik, vT, iwT = _proj_rope(xs, wT, bT, cosT, sinT)
            dsa = _dsa(qT, iqT, iwT, ik, k, vT)
            x1, gates = _mix(xs, dsa, w_mix, b_mix, pool_w[l].astype(BF16), pool_scale[l][None, :],
                             mk, mv, w_br[l].astype(BF16), w_out[l].astype(BF16),
                             ln1_g[l][None, :], ln1_b[l][None, :], wr, rb)
            xs = _moe(x1, gates, w1[l].astype(BF16), w3[l].astype(BF16), w2[l].astype(BF16),
                      ln2_g[l][None, :], ln2_b[l][None, :])
        outs.append(xs)
    return jnp.stack(outs, axis=0)
```

```python
import functools

import jax
import jax.numpy as jnp
from jax import lax
from jax.experimental import pallas as pl
from jax.experimental.pallas import tpu as pltpu

D_MODEL = 1024
DEPTH = 2
CHUNK = 64
ROPE_THETA = 10000.0
LN_EPS = 1e-5
POOL_WINDOWS = (2, 4, 8, 16)
POOL_GROUP_DIM = 128
POOL_HALO = 16
ATT_HEADS = 8
HEAD_DIM = 64
ROPE_HALF = HEAD_DIM // 2
V_ROWS = HEAD_DIM + 16
LOG2_E = 1.4426950408889634
IDX_HEADS = 8
TOPK_MAX = 256
MEM_HEADS = 4
MEM_HEAD_DIM = 128
BRANCH_WIDTH = 512
N_BRANCH = 3
N_EXPERTS = 16
N_GROUPS = 4
EXPERTS_PER_GROUP = 4
D_EXPERT = 512
DN_ALPHA = (2 * DEPTH) ** 0.25

LANES = 128
INT_MIN = -(2 ** 31)
HALF_OFFSET = 2 ** 15
NEG_BIG = -1e30

PROJ_TM = 512
DSA_TQ = 256
DSA_TK = 512
MIX_TM = 512
MOE_TM = 1024
VMEM_LIMIT = 56 * 1024 * 1024

F32 = jnp.float32
BF16 = jnp.bfloat16


def _dot(a, b):
    return jnp.dot(a, b, preferred_element_type=F32)


def _dot_nt(a, b):
    return lax.dot_general(a, b, (((1,), (1,)), ((), ())), preferred_element_type=F32)


def _layer_norm(y, g, b):
    mu = jnp.mean(y, axis=-1, keepdims=True)
    yc = y - mu
    var = jnp.mean(yc * yc, axis=-1, keepdims=True)
    return yc * lax.rsqrt(var + LN_EPS) * g + b


def _sigmoid(x):
    return 1.0 / (1.0 + jnp.exp(-x))


def _matmul_kernel(a_ref, b_ref, o_ref):
    o_ref[...] = _dot(a_ref[...].astype(BF16), b_ref[...])


def _matmul(a, b):
    m, n = a.shape[0], b.shape[1]
    return pl.pallas_call(
        _matmul_kernel,
        out_shape=jax.ShapeDtypeStruct((m, n), F32),
        name="mem_kv",
    )(a, b)


_R_Q, _R_IQ, _R_K, _R_IK, _R_V, _R_IW, _R_END = 0, 512, 1024, 1536, 1600, 2112, 2128


def _proj_rope_kernel(x_ref, w_ref, b_ref, cos_ref, sin_ref,
                      qT_ref, iqT_ref, k_ref, ik_ref, vT_ref, iwT_ref, kT_scr, ikT_scr):
    xb = x_ref[...].astype(BF16)
    cos = cos_ref[...]
    sin = sin_ref[...]

    def proj(r0, r1):
        return _dot_nt(w_ref[r0:r1, :], xb) + b_ref[r0:r1, :]

    def rope_store(r, dst_ref, dtype):
        for h in range(r.shape[0] // HEAD_DIM):
            a = h * HEAD_DIM
            x1 = r[a:a + ROPE_HALF]
            x2 = r[a + ROPE_HALF:a + HEAD_DIM]
            dst_ref[a:a + ROPE_HALF, :] = (x1 * cos - x2 * sin).astype(dtype)
            dst_ref[a + ROPE_HALF:a + HEAD_DIM, :] = (x2 * cos + x1 * sin).astype(dtype)

    rope_store(proj(_R_Q, _R_IQ), qT_ref, BF16)
    rope_store(proj(_R_IQ, _R_K), iqT_ref, BF16)
    rope_store(proj(_R_K, _R_IK), kT_scr, F32)
    k_ref[...] = kT_scr[...].T.astype(BF16)
    ikT_scr[HEAD_DIM:, :] = jnp.zeros((LANES - HEAD_DIM, ikT_scr.shape[1]), F32)
    rope_store(proj(_R_IK, _R_V), ikT_scr, F32)
    ik_ref[...] = ikT_scr[...].T.astype(BF16)
    v = proj(_R_V, _R_IW).astype(BF16)
    tm = v.shape[1]
    pad_row = lax.broadcasted_iota(jnp.int32, (V_ROWS - HEAD_DIM, tm), 0)
    ones_then_zeros = jnp.where(pad_row == 0, 1.0, 0.0).astype(BF16)
    for h in range(ATT_HEADS):
        a0 = h * V_ROWS
        vT_ref[a0:a0 + HEAD_DIM, :] = v[h * HEAD_DIM:(h + 1) * HEAD_DIM]
        vT_ref[a0 + HEAD_DIM:a0 + V_ROWS, :] = ones_then_zeros
    iwT_ref[...] = proj(_R_IW, _R_END)[0:IDX_HEADS]


def _proj_rope(x2, wT, bT, cosT, sinT):
    s = x2.shape[0]
    tm = min(PROJ_TM, s)
    col = lambda i: (0, i)
    return pl.pallas_call(
        _proj_rope_kernel,
        grid=(s // tm,),
        in_specs=[
            pl.BlockSpec((tm, D_MODEL), lambda i: (i, 0)),
            pl.BlockSpec((_R_END, D_MODEL), lambda i: (0, 0)),
            pl.BlockSpec((_R_END, 1), lambda i: (0, 0)),
            pl.BlockSpec((ROPE_HALF, tm), col),
            pl.BlockSpec((ROPE_HALF, tm), col),
        ],
        out_specs=[
            pl.BlockSpec((512, tm), col),
            pl.BlockSpec((512, tm), col),
            pl.BlockSpec((tm, 512), lambda i: (i, 0)),
            pl.BlockSpec((tm, LANES), lambda i: (i, 0)),
            pl.BlockSpec((ATT_HEADS * V_ROWS, tm), col),
            pl.BlockSpec((IDX_HEADS, tm), col),
        ],
        out_shape=[
            jax.ShapeDtypeStruct((512, s), BF16),
            jax.ShapeDtypeStruct((512, s), BF16),
            jax.ShapeDtypeStruct((s, 512), BF16),
            jax.ShapeDtypeStruct((s, LANES), BF16),
            jax.ShapeDtypeStruct((ATT_HEADS * V_ROWS, s), BF16),
            jax.ShapeDtypeStruct((IDX_HEADS, s), F32),
        ],
        scratch_shapes=[pltpu.VMEM((512, tm), F32), pltpu.VMEM((LANES, tm), F32)],
        compiler_params=pltpu.CompilerParams(
            dimension_semantics=("parallel",), vmem_limit_bytes=VMEM_LIMIT),
        name="proj_rope",
    )(x2, wT, bT, cosT, sinT)


def _dsa_kernel(qT_ref, iqT_ref, iwT_ref, ik_ref, k_hbm, vT_hbm, o_ref,
                hi_ref, lo_ref, kbuf, vbuf, sem, iqp_ref, qp_ref, m_ref, tmx_ref, acc_ref, out_ref,
                sa_ref, sb_ref, *, n_sel):
    tq = qT_ref.shape[1]
    tk = kbuf.shape[2]
    i = pl.program_id(0)
    q0 = i * tq
    n_kt = (q0 + tq + tk - 1) // tk

    zpad = jnp.zeros((LANES - HEAD_DIM, tq), BF16)
    for h in range(IDX_HEADS):
        iqp_ref[h, 0:HEAD_DIM, :] = iqT_ref[h * HEAD_DIM:(h + 1) * HEAD_DIM, :]
        iqp_ref[h, HEAD_DIM:, :] = zpad
    row = lax.broadcasted_iota(jnp.int32, (256, tq), 0)
    for h in range(ATT_HEADS):
        j, hh = h // 4, h % 4
        quad = qT_ref[256 * j:256 * (j + 1), :]
        keep = (row >= hh * HEAD_DIM) & (row < (hh + 1) * HEAD_DIM)
        qp_ref[h] = jnp.where(keep, quad, jnp.zeros_like(quad))

    q_chunk = (q0 + lax.broadcasted_iota(jnp.int32, (1, tq), 1)) // CHUNK

    def score_tile(kt, mask_chunks):
        ks = pl.multiple_of(kt * tk, tk)
        ikt = ik_ref[pl.ds(ks, tk), :]
        acc = jnp.zeros((tk, tq), F32)
        for h in range(IDX_HEADS):
            z = _dot(ikt, iqp_ref[h])
            acc = acc + jnp.maximum(z, 0.0) * iwT_ref[h:h + 1, :]
        acc = acc + 0.0
        bits = pltpu.bitcast(acc, jnp.int32)
        skey = bits ^ ((bits >> 31) & jnp.int32(0x7FFFFFFF))
        if mask_chunks:
            key_chunk = (ks + lax.broadcasted_iota(jnp.int32, (tk, 1), 0)) // CHUNK
            skey = jnp.where(key_chunk <= q_chunk, skey, jnp.int32(INT_MIN))
        hi_ref[pl.ds(ks, tk), :] = (skey >> 16).astype(jnp.int16)
        lo_ref[pl.ds(ks, tk), :] = ((skey & jnp.int32(0xFFFF)) - HALF_OFFSET).astype(jnp.int16)

    lax.fori_loop(0, n_kt - 1, lambda kt, c: (score_tile(kt, False), c)[1], 0)
    score_tile(n_kt - 1, True)

    one16, zero16 = jnp.int16(1), jnp.int16(0)

    def add_rows(c, flags):
        parts = [flags[16 * r:16 * (r + 1), :] for r in range(tk // 16)]
        while len(parts) > 1:
            parts = [a + b for a, b in zip(parts[0::2], parts[1::2])]
        return c + parts[0]

    def total(c):
        return jnp.sum(c.astype(jnp.int32), axis=0, keepdims=True)

    def select16(plane_ref, need, count_all):
        def step(b, carry):
            ans_u, cnt_ans = carry
            cand_u = ans_u | (jnp.int32(1) << (15 - b))
            cand = (cand_u - HALF_OFFSET).astype(jnp.int16)

            def body(kt, c):
                ks = pl.multiple_of(kt * tk, tk)
                return add_rows(c, jnp.where(plane_ref[pl.ds(ks, tk), :] >= cand, one16, zero16))

            cnt = total(lax.fori_loop(0, n_kt, body, jnp.zeros((16, tq), jnp.int16)))
            ok = cnt >= need
            return jnp.where(ok, cand_u, ans_u), jnp.where(ok, cnt, cnt_ans)

        return lax.fori_loop(0, 16, step, (jnp.zeros((1, tq), jnp.int32), count_all))

    zero_row = jnp.zeros((1, tq), jnp.int32)
    hi_u, _ = select16(hi_ref, n_sel, zero_row)
    hi_t = (hi_u - HALF_OFFSET).astype(jnp.int16)

    def restrict_low(kt, carry):
        c_gt, c_eq = carry
        ks = pl.multiple_of(kt * tk, tk)
        hi = hi_ref[pl.ds(ks, tk), :]
        at = hi == hi_t
        lo_ref[pl.ds(ks, tk), :] = jnp.where(at, lo_ref[pl.ds(ks, tk), :], jnp.int16(-HALF_OFFSET))
        return (add_rows(c_gt, jnp.where(hi > hi_t, one16, zero16)),
                add_rows(c_eq, jnp.where(at, one16, zero16)))

    zero_cnt = jnp.zeros((16, tq), jnp.int16)
    c_gt, c_eq = lax.fori_loop(0, n_kt, restrict_low, (zero_cnt, zero_cnt))
    n_above = total(c_gt)
    lo_u, n_at = select16(lo_ref, n_sel - n_above, total(c_eq))
    short = hi_u == 0
    lo_u = jnp.where(short & (lo_u == 0), 1, lo_u)
    lo_t = (lo_u - HALF_OFFSET).astype(jnp.int16)

    excess = jnp.where(short, 0, n_above + n_at - n_sel)

    @pl.when(jnp.max(excess) > 0)
    def _():
        def tied(kt):
            ks = pl.multiple_of(kt * tk, tk)
            eq = (hi_ref[pl.ds(ks, tk), :] == hi_t) & (lo_ref[pl.ds(ks, tk), :] == lo_t)
            return ks, jnp.where(eq, jnp.ones((), BF16), jnp.zeros((), BF16))

        def count_tied(kt, c):
            return c + jnp.sum(tied(kt)[1].astype(F32), axis=0, keepdims=True)

        n_tied = lax.fori_loop(0, n_kt, count_tied, jnp.zeros((1, tq), F32))
        keep = n_tied - excess.astype(F32)
        r_i = lax.broadcasted_iota(jnp.int32, (tk, tk), 0)
        c_i = lax.broadcasted_iota(jnp.int32, (tk, tk), 1)
        before = jnp.where(c_i < r_i, 1.0, 0.0).astype(BF16)

        def drop_late(kt, seen):
            ks, eq = tied(kt)
            rank = _dot(before, eq) + seen
            eq_f = eq.astype(F32)
            drop = jnp.where(rank >= keep, eq_f, 0.0).astype(BF16)
            gone = jnp.int16(-HALF_OFFSET)
            hi_ref[pl.ds(ks, tk), :] = jnp.where(drop > 0, gone, hi_ref[pl.ds(ks, tk), :])
            lo_ref[pl.ds(ks, tk), :] = jnp.where(drop > 0, gone, lo_ref[pl.ds(ks, tk), :])
            return seen + jnp.sum(eq_f, axis=0, keepdims=True)

        lax.fori_loop(0, n_kt, drop_late, jnp.zeros((1, tq), F32))

    n2 = (n_kt + 1) // 2
    last = 2 * n2 - 1

    @pl.when(n_kt % 2 == 1)
    def _():
        sentinel = jnp.full((tk, tq), -HALF_OFFSET, jnp.int16)
        hi_ref[pl.ds(pl.multiple_of(n_kt * tk, tk), tk), :] = sentinel
        lo_ref[pl.ds(pl.multiple_of(n_kt * tk, tk), tk), :] = sentinel

    def k_copy(t, par, w):
        return pltpu.make_async_copy(k_hbm.at[pl.ds(t * tk, tk), :], kbuf.at[par, w], sem.at[0, par, w])

    def v_copy(t, par, w):
        return pltpu.make_async_copy(vT_hbm.at[:, pl.ds(t * tk, tk)], vbuf.at[par, w], sem.at[1, par, w])

    def pair_copies(j, par):
        return (k_copy(2 * j + 1, par, 0), k_copy(jnp.minimum(2 * j + 2, last), par, 1),
                v_copy(2 * j, par, 0), v_copy(2 * j + 1, par, 1))

    def logits_head(kb, h, bias, s_dst, tmx_dst):
        j = h // 4
        s = _dot(kb[:, 256 * j:256 * (j + 1)], qp_ref[h]) + bias
        s_dst[h] = s
        tmx_dst[h:h + 1, :] = jnp.max(s, axis=0, keepdims=True)

    def mask_bias(t):
        ks = pl.multiple_of(t * tk, tk)
        hi = hi_ref[pl.ds(ks, tk), :]
        sel = (hi > hi_t) | ((hi == hi_t) & (lo_ref[pl.ds(ks, tk), :] >= lo_t))
        return jnp.where(sel, jnp.zeros((), BF16), jnp.full((), NEG_BIG, BF16)).astype(F32)

    def attend_head(vb, h, s_src, tmx_src):
        m_old = m_ref[h:h + 1, :]
        m_new = jnp.maximum(m_old, tmx_src[h:h + 1, :])
        alpha = jnp.exp2(m_old - m_new)
        p = jnp.exp2(s_src[h] - m_new)
        a0 = h * V_ROWS
        pv = _dot(vb[a0:a0 + V_ROWS, :], p.astype(BF16))
        acc_ref[a0:a0 + V_ROWS, :] = alpha * acc_ref[a0:a0 + V_ROWS, :] + pv
        m_ref[h:h + 1, :] = m_new

    def half_step(kb, t_next, s_dst, tmx_dst, vb, s_src, tmx_src):
        bias = mask_bias(t_next)
        for h in range(ATT_HEADS):
            logits_head(kb, h, bias, s_dst, tmx_dst)
            attend_head(vb, h, s_src, tmx_src)

    m_ref[...] = jnp.full(m_ref.shape, NEG_BIG, F32)
    acc_ref[...] = jnp.zeros(acc_ref.shape, F32)

    first = k_copy(0, 1, 1)
    first.start()
    for c in pair_copies(0, 0):
        c.start()
    first.wait()
    bias0 = mask_bias(0)
    for h in range(ATT_HEADS):
        logits_head(kbuf.at[1, 1], h, bias0, sa_ref, tmx_ref.at[0])

    def pair_step(j, carry):
        par = j % 2
        for c in pair_copies(j, par):
            c.wait()

        @pl.when(j + 1 < n2)
        def _():
            for c in pair_copies(j + 1, 1 - par):
                c.start()

        half_step(kbuf.at[par, 0], 2 * j + 1, sb_ref, tmx_ref.at[1], vbuf.at[par, 0], sa_ref, tmx_ref.at[0])
        half_step(kbuf.at[par, 1], jnp.minimum(2 * j + 2, last), sa_ref, tmx_ref.at[0],
                  vbuf.at[par, 1], sb_ref, tmx_ref.at[1])
        return carry

    lax.fori_loop(0, n2, pair_step, 0)

    for h in range(ATT_HEADS):
        a0 = h * V_ROWS
        den = acc_ref[a0 + HEAD_DIM:a0 + HEAD_DIM + 1, :]
        out_ref[h * HEAD_DIM:(h + 1) * HEAD_DIM, :] = acc_ref[a0:a0 + HEAD_DIM, :] / den
    o_ref[...] = out_ref[...].T.astype(BF16)


def _dsa(qT, iqT, iwT, ik, k, vT):
    s = k.shape[0]
    tq = min(DSA_TQ, s)
    tk = min(DSA_TK, s // 2)
    assert s % (2 * tk) == 0 and s % tq == 0 and tk % tq == 0
    n_sel = min(TOPK_MAX, s // 4)
    col = lambda i: (0, i)
    return pl.pallas_call(
        functools.partial(_dsa_kernel, n_sel=n_sel),
        grid=(s // tq,),
        in_specs=[
            pl.BlockSpec((512, tq), col),
            pl.BlockSpec((512, tq), col),
            pl.BlockSpec((IDX_HEADS, tq), col),
            pl.BlockSpec((s, LANES), lambda i: (0, 0)),
            pl.BlockSpec(memory_space=pl.ANY),
            pl.BlockSpec(memory_space=pl.ANY),
        ],
        out_specs=pl.BlockSpec((tq, 512), lambda i: (i, 0)),
        out_shape=jax.ShapeDtypeStruct((s, 512), BF16),
        scratch_shapes=[
            pltpu.VMEM((s, tq), jnp.int16),
            pltpu.VMEM((s, tq), jnp.int16),
            pltpu.VMEM((2, 2, tk, 512), BF16),
            pltpu.VMEM((2, 2, ATT_HEADS * V_ROWS, tk), BF16),
            pltpu.SemaphoreType.DMA((2, 2, 2)),
            pltpu.VMEM((IDX_HEADS, LANES, tq), BF16),
            pltpu.VMEM((ATT_HEADS, 256, tq), BF16),
            pltpu.VMEM((ATT_HEADS, tq), F32),
            pltpu.VMEM((2, ATT_HEADS, tq), F32),
            pltpu.VMEM((ATT_HEADS * V_ROWS, tq), F32),
            pltpu.VMEM((512, tq), F32),
            pltpu.VMEM((ATT_HEADS, tk, tq), F32),
            pltpu.VMEM((ATT_HEADS, tk, tq), F32),
        ],
        compiler_params=pltpu.CompilerParams(
            dimension_semantics=("arbitrary",), vmem_limit_bytes=VMEM_LIMIT),
        name="dsa",
    )(qT, iqT, iwT, ik, k, vT)


_C_POOL, _C_MQ, _C_GATE = 0, 512, 1024


def _route(logitsT, bias):
    sc = _sigmoid(logitsT)
    bi = sc + bias
    s_rows = [sc[e:e + 1, :] for e in range(N_EXPERTS)]
    b_rows = [bi[e:e + 1, :] for e in range(N_EXPERTS)]
    best = None
    best_score = None
    for g in range(N_GROUPS):
        v = b_rows[4 * g:4 * g + 4]
        top2 = None
        for a in range(4):
            for c in range(a + 1, 4):
                pair = v[a] + v[c]
                top2 = pair if top2 is None else jnp.maximum(top2, pair)
        if best is None:
            best, best_score = jnp.zeros_like(top2, dtype=jnp.int32), top2
        else:
            upd = top2 > best_score
            best = jnp.where(upd, g, best)
            best_score = jnp.where(upd, top2, best_score)
    wb, ws = [], []
    for m in range(EXPERTS_PER_GROUP):
        vb, vs = b_rows[m], s_rows[m]
        for g in range(1, N_GROUPS):
            vb = jnp.where(best == g, b_rows[4 * g + m], vb)
            vs = jnp.where(best == g, s_rows[4 * g + m], vs)
        wb.append(vb)
        ws.append(vs)
    i1, v1 = jnp.zeros_like(best), wb[0]
    for m in range(1, 4):
        upd = wb[m] > v1
        i1 = jnp.where(upd, m, i1)
        v1 = jnp.where(upd, wb[m], v1)
    i2, v2 = None, None
    for m in range(4):
        cand = jnp.where(i1 == m, -jnp.inf, wb[m])
        if i2 is None:
            i2, v2 = jnp.zeros_like(best), cand
        else:
            upd = cand > v2
            i2 = jnp.where(upd, m, i2)
            v2 = jnp.where(upd, cand, v2)
    s1 = ws[0]
    s2 = ws[0]
    for m in range(1, 4):
        s1 = jnp.where(i1 == m, ws[m], s1)
        s2 = jnp.where(i2 == m, ws[m], s2)
    tot = s1 + s2
    g1, g2 = s1 / tot, s2 / tot
    rows = []
    for e in range(N_EXPERTS):
        g, m = e // 4, e % 4
        val = jnp.where(i1 == m, g1, jnp.where(i2 == m, g2, 0.0))
        rows.append(jnp.where(best == g, val, 0.0))
    return rows


def _mix_kernel(x_ref, xh_ref, dsa_ref, w_ref, b_ref, pw_ref, ps_ref, mk_ref, mv_ref,
                wbr_ref, wout_ref, g_ref, beta_ref, wr_ref, rb_ref,
                x1_ref, gate_ref, uext_ref):
    i = pl.program_id(0)
    tm = x_ref.shape[0]
    x = x_ref[...]
    xb = x.astype(BF16)

    def proj(c0, c1):
        return _dot(xb, w_ref[:, c0:c1]) + b_ref[:, c0:c1]

    u = proj(_C_POOL, _C_MQ)
    uh = _dot(xh_ref[...].astype(BF16), w_ref[:, _C_POOL:_C_MQ]) + b_ref[:, _C_POOL:_C_MQ]
    uext_ref[0:POOL_HALO, :] = jnp.where(i > 0, uh, 0.0)
    uext_ref[POOL_HALO:, :] = u
    t = i * tm + lax.broadcasted_iota(jnp.int32, (tm, 1), 0)
    pooled = []
    for g, w in enumerate(POOL_WINDOWS):
        c0 = g * POOL_GROUP_DIM
        ug = u[:, c0:c0 + POOL_GROUP_DIM]
        win = ug
        for j in range(1, w):
            win = win + uext_ref[POOL_HALO - j:POOL_HALO - j + tm, c0:c0 + POOL_GROUP_DIM]
        cnt = jnp.minimum(t + 1, w).astype(F32)
        d = win / cnt - ug
        pooled.append(_dot(d.astype(BF16), pw_ref[g]))
    pool_out = jnp.concatenate(pooled, axis=-1) * ps_ref[...]

    mq = proj(_C_MQ, _C_GATE)
    mem = []
    for h in range(MEM_HEADS):
        c0 = h * MEM_HEAD_DIM
        s = _dot_nt(mq[:, c0:c0 + MEM_HEAD_DIM].astype(BF16), mk_ref[:, c0:c0 + MEM_HEAD_DIM])
        s = s * (MEM_HEAD_DIM ** -0.5)
        s = s - jnp.max(s, axis=-1, keepdims=True)
        p = jnp.exp(s)
        p = p / jnp.sum(p, axis=-1, keepdims=True)
        mem.append(_dot(p.astype(BF16), mv_ref[:, c0:c0 + MEM_HEAD_DIM]))
    mem_out = jnp.concatenate(mem, axis=-1)

    merged = jnp.zeros((tm, D_MODEL), F32)
    for n, br in enumerate((pool_out.astype(BF16), dsa_ref[...], mem_out.astype(BF16))):
        gate = _sigmoid(proj(_C_GATE + n * D_MODEL, _C_GATE + (n + 1) * D_MODEL))
        merged = merged + gate * _dot(br, wbr_ref[n])
    mix = _dot(merged.astype(BF16), wout_ref[...])
    x1 = _layer_norm(DN_ALPHA * x + mix, g_ref[...], beta_ref[...])
    x1_ref[...] = x1

    logitsT = _dot_nt(wr_ref[...], x1.astype(BF16))
    rows = _route(logitsT, rb_ref[...])
    gT = jnp.concatenate(rows + [jnp.zeros((LANES - N_EXPERTS, tm), F32)], axis=0)
    gate_ref[...] = gT.T


def _mix(x2, dsa, w, b, pw, ps, mk, mv, wbr, wout, g, beta, wr, rb):
    s = x2.shape[0]
    tm = min(MIX_TM, s)
    const2 = lambda i: (0, 0)
    const3 = lambda i: (0, 0, 0)
    hb = tm // POOL_HALO
    return pl.pallas_call(
        _mix_kernel,
        grid=(s // tm,),
        in_specs=[
            pl.BlockSpec((tm, D_MODEL), lambda i: (i, 0)),
            pl.BlockSpec((POOL_HALO, D_MODEL), lambda i: (jnp.maximum(i * hb - 1, 0), 0)),
            pl.BlockSpec((tm, BRANCH_WIDTH), lambda i: (i, 0)),
            pl.BlockSpec(w.shape, const2),
            pl.BlockSpec(b.shape, const2),
            pl.BlockSpec(pw.shape, const3),
            pl.BlockSpec(ps.shape, const2),
            pl.BlockSpec(mk.shape, const2),
            pl.BlockSpec(mv.shape, const2),
            pl.BlockSpec(wbr.shape, const3),
            pl.BlockSpec(wout.shape, const2),
            pl.BlockSpec(g.shape, const2),
            pl.BlockSpec(beta.shape, const2),
            pl.BlockSpec(wr.shape, const2),
            pl.BlockSpec(rb.shape, const2),
        ],
        out_specs=[
            pl.BlockSpec((tm, D_MODEL), lambda i: (i, 0)),
            pl.BlockSpec((tm, LANES), lambda i: (i, 0)),
        ],
        out_shape=[
            jax.ShapeDtypeStruct((s, D_MODEL), F32),
            jax.ShapeDtypeStruct((s, LANES), F32),
        ],
        scratch_shapes=[pltpu.VMEM((POOL_HALO + tm, BRANCH_WIDTH), F32)],
        compiler_params=pltpu.CompilerParams(
            dimension_semantics=("parallel",), vmem_limit_bytes=VMEM_LIMIT),
        name="mix",
    )(x2, x2, dsa, w, b, pw, ps, mk, mv, wbr, wout, g, beta, wr, rb)


def _moe_kernel(x_ref, gate_ref, w1_ref, w3_ref, w2_ref, g_ref, beta_ref, o_ref, xb_ref, acc_ref):
    e = pl.program_id(1)

    @pl.when(e == 0)
    def _():
        xb_ref[...] = x_ref[...].astype(BF16)
        acc_ref[...] = jnp.zeros(acc_ref.shape, F32)

    xb = xb_ref[...]
    h1 = _dot(xb, w1_ref[0])
    h3 = _dot(xb, w3_ref[0])
    h = (h1 * _sigmoid(h1)) * h3
    y = _dot(h.astype(BF16), w2_ref[0])
    lane = lax.broadcasted_iota(jnp.int32, gate_ref.shape, 1)
    ge = jnp.sum(jnp.where(lane == e, gate_ref[...], 0.0), axis=-1, keepdims=True)
    acc_ref[...] += y * ge

    @pl.when(e == N_EXPERTS - 1)
    def _():
        o_ref[...] = _layer_norm(DN_ALPHA * x_ref[...] + acc_ref[...], g_ref[...], beta_ref[...])


def _moe(x1, gates, w1, w3, w2, g, beta):
    s = x1.shape[0]
    tm = min(MOE_TM, s)
    const2 = lambda i, e: (0, 0)
    return pl.pallas_call(
        _moe_kernel,
        grid=(s // tm, N_EXPERTS),
        in_specs=[
            pl.BlockSpec((tm, D_MODEL), lambda i, e: (i, 0)),
            pl.BlockSpec((tm, LANES), lambda i, e: (i, 0)),
            pl.BlockSpec((1, D_MODEL, D_EXPERT), lambda i, e: (e, 0, 0)),
            pl.BlockSpec((1, D_MODEL, D_EXPERT), lambda i, e: (e, 0, 0)),
            pl.BlockSpec((1, D_EXPERT, D_MODEL), lambda i, e: (e, 0, 0)),
            pl.BlockSpec(g.shape, const2),
            pl.BlockSpec(beta.shape, const2),
        ],
        out_specs=pl.BlockSpec((tm, D_MODEL), lambda i, e: (i, 0)),
        out_shape=jax.ShapeDtypeStruct((s, D_MODEL), F32),
        scratch_shapes=[pltpu.VMEM((tm, D_MODEL), BF16), pltpu.VMEM((tm, D_MODEL), F32)],
        compiler_params=pltpu.CompilerParams(
            dimension_semantics=("parallel", "arbitrary"), vmem_limit_bytes=VMEM_LIMIT),
        name="moe",
    )(x1, gates, w1, w3, w2, g, beta)


def _split_w_in(w_in, b_in):
    o = [0, 512, 1024, 1536, 2048, 2560, 2624, 2632, 3144, 6216]
    pool, q, k, v, iq, ik, iw, mq, gates = [(w_in[:, a:c], b_in[a:c]) for a, c in zip(o[:-1], o[1:])]
    qs = HEAD_DIM ** -0.5 * LOG2_E
    iws = (HEAD_DIM ** -0.5) * (IDX_HEADS ** -0.5)
    pad = _R_END - _R_IW - IDX_HEADS
    wT = jnp.concatenate([q[0] * qs, iq[0], k[0], ik[0], v[0], iw[0] * iws,
                          jnp.zeros((D_MODEL, pad), F32)], axis=1).T
    bT = jnp.concatenate([q[1] * qs, iq[1], k[1], ik[1], v[1], iw[1] * iws, jnp.zeros((pad,), F32)])
    w_mix = jnp.concatenate([pool[0], mq[0], gates[0]], axis=1)
    b_mix = jnp.concatenate([pool[1], mq[1], gates[1]])
    return wT.astype(BF16), bT[:, None], w_mix.astype(BF16), b_mix[None, :]


def kernel(x, mem, positions, w_in, b_in, pool_w, pool_scale, w_mem_kv, w_br, w_out, ln1_g, ln1_b,
           w_router, router_bias, w1, w3, w2, ln2_g, ln2_b):
    B, S, D = x.shape
    inv_freq = ROPE_THETA ** (-jnp.arange(ROPE_HALF, dtype=F32) / ROPE_HALF)
    wr = w_router.T.astype(BF16)
    rb = router_bias.astype(F32)[:, None]
    outs = []
    for bi in range(B):
        xs = x[bi]
        ang = positions[bi].astype(F32)[None, :] * inv_freq[:, None]
        cosT, sinT = jnp.cos(ang), jnp.sin(ang)
        for l in range(DEPTH):
            wT, bT, w_mix, b_mix = _split_w_in(w_in[l], b_in[l])
            kv = _matmul(mem[bi], w_mem_kv[l].astype(BF16))
            mk = kv[:, :BRANCH_WIDTH].astype(BF16)
            mv = kv[:, BRANCH_WIDTH:].astype(BF16)
            qT, iqT, k, ik, vT, iwT = _proj_rope(xs, wT, bT, cosT, sinT)
            dsa = _dsa(qT, iqT, iwT, ik, k, vT)
            x1, gates = _mix(xs, dsa, w_mix, b_mix, pool_w[l].astype(BF16), pool_scale[l][None, :],
                             mk, mv, w_br[l].astype(BF16), w_out[l].astype(BF16),
                             ln1_g[l][None, :], ln1_b[l][None, :], wr, rb)
            xs = _moe(x1, gates, w1[l].astype(BF16), w3[l].astype(BF16), w2[l].astype(BF16),
                      ln2_g[l][None, :], ln2_b[l][None, :])
        outs.append(xs)
    return jnp.stack(outs, axis=0)
```

```python
import functools

import jax
import jax.numpy as jnp
from jax import lax
from jax.experimental import pallas as pl
from jax.experimental.pallas import tpu as pltpu

D_MODEL = 1024
DEPTH = 2
CHUNK = 64
ROPE_THETA = 10000.0
LN_EPS = 1e-5
POOL_WINDOWS = (2, 4, 8, 16)
POOL_GROUP_DIM = 128
POOL_HALO = 16
ATT_HEADS = 8
HEAD_DIM = 64
ROPE_HALF = HEAD_DIM // 2
V_ROWS = HEAD_DIM + 16
LOG2_E = 1.4426950408889634
IDX_HEADS = 8
TOPK_MAX = 256
MEM_HEADS = 4
MEM_HEAD_DIM = 128
BRANCH_WIDTH = 512
N_BRANCH = 3
N_EXPERTS = 16
N_GROUPS = 4
EXPERTS_PER_GROUP = 4
D_EXPERT = 512
DN_ALPHA = (2 * DEPTH) ** 0.25

LANES = 128
INT_MIN = -(2 ** 31)
HALF_OFFSET = 2 ** 15
NEG_BIG = -1e30

PROJ_TM = 512
DSA_TQ = 256
DSA_TK = 512
MIX_TM = 1024
MOE_TM = 1024
VMEM_LIMIT = 56 * 1024 * 1024

F32 = jnp.float32
BF16 = jnp.bfloat16


def _dot(a, b):
    return jnp.dot(a, b, preferred_element_type=F32)


def _dot_nt(a, b):
    return lax.dot_general(a, b, (((1,), (1,)), ((), ())), preferred_element_type=F32)


def _layer_norm(y, g, b):
    mu = jnp.mean(y, axis=-1, keepdims=True)
    yc = y - mu
    var = jnp.mean(yc * yc, axis=-1, keepdims=True)
    return yc * lax.rsqrt(var + LN_EPS) * g + b


def _sigmoid(x):
    return 1.0 / (1.0 + jnp.exp(-x))


def _matmul_kernel(a_ref, b_ref, o_ref):
    o_ref[...] = _dot(a_ref[...].astype(BF16), b_ref[...])


def _matmul(a, b):
    m, n = a.shape[0], b.shape[1]
    return pl.pallas_call(
        _matmul_kernel,
        out_shape=jax.ShapeDtypeStruct((m, n), F32),
        name="mem_kv",
    )(a, b)


_R_Q, _R_IQ, _R_K, _R_IK, _R_V, _R_IW, _R_END = 0, 512, 1024, 1536, 1600, 2112, 2128


def _proj_rope_kernel(x_ref, w_ref, b_ref, cos_ref, sin_ref,
                      qT_ref, iqT_ref, k_ref, ik_ref, vT_ref, iwT_ref, kT_scr, ikT_scr):
    xb = x_ref[...].astype(BF16)
    cos = cos_ref[...]
    sin = sin_ref[...]

    def proj(r0, r1):
        return _dot_nt(w_ref[r0:r1, :], xb) + b_ref[r0:r1, :]

    def rope_store(r, dst_ref, dtype):
        for h in range(r.shape[0] // HEAD_DIM):
            a = h * HEAD_DIM
            x1 = r[a:a + ROPE_HALF]
            x2 = r[a + ROPE_HALF:a + HEAD_DIM]
            dst_ref[a:a + ROPE_HALF, :] = (x1 * cos - x2 * sin).astype(dtype)
            dst_ref[a + ROPE_HALF:a + HEAD_DIM, :] = (x2 * cos + x1 * sin).astype(dtype)

    rope_store(proj(_R_Q, _R_IQ), qT_ref, BF16)
    rope_store(proj(_R_IQ, _R_K), iqT_ref, BF16)
    rope_store(proj(_R_K, _R_IK), kT_scr, F32)
    k_ref[...] = kT_scr[...].T.astype(BF16)
    ikT_scr[HEAD_DIM:, :] = jnp.zeros((LANES - HEAD_DIM, ikT_scr.shape[1]), F32)
    rope_store(proj(_R_IK, _R_V), ikT_scr, F32)
    ik_ref[...] = ikT_scr[...].T.astype(BF16)
    v = proj(_R_V, _R_IW).astype(BF16)
    tm = v.shape[1]
    pad_row = lax.broadcasted_iota(jnp.int32, (V_ROWS - HEAD_DIM, tm), 0)
    ones_then_zeros = jnp.where(pad_row == 0, 1.0, 0.0).astype(BF16)
    for h in range(ATT_HEADS):
        a0 = h * V_ROWS
        vT_ref[a0:a0 + HEAD_DIM, :] = v[h * HEAD_DIM:(h + 1) * HEAD_DIM]
        vT_ref[a0 + HEAD_DIM:a0 + V_ROWS, :] = ones_then_zeros
    iwT_ref[...] = proj(_R_IW, _R_END)[0:IDX_HEADS]


def _proj_rope(x2, wT, bT, cosT, sinT):
    s = x2.shape[0]
    tm = min(PROJ_TM, s)
    col = lambda i: (0, i)
    return pl.pallas_call(
        _proj_rope_kernel,
        grid=(s // tm,),
        in_specs=[
            pl.BlockSpec((tm, D_MODEL), lambda i: (i, 0)),
            pl.BlockSpec((_R_END, D_MODEL), lambda i: (0, 0)),
            pl.BlockSpec((_R_END, 1), lambda i: (0, 0)),
            pl.BlockSpec((ROPE_HALF, tm), col),
            pl.BlockSpec((ROPE_HALF, tm), col),
        ],
        out_specs=[
            pl.BlockSpec((512, tm), col),
            pl.BlockSpec((512, tm), col),
            pl.BlockSpec((tm, 512), lambda i: (i, 0)),
            pl.BlockSpec((tm, LANES), lambda i: (i, 0)),
            pl.BlockSpec((ATT_HEADS * V_ROWS, tm), col),
            pl.BlockSpec((IDX_HEADS, tm), col),
        ],
        out_shape=[
            jax.ShapeDtypeStruct((512, s), BF16),
            jax.ShapeDtypeStruct((512, s), BF16),
            jax.ShapeDtypeStruct((s, 512), BF16),
            jax.ShapeDtypeStruct((s, LANES), BF16),
            jax.ShapeDtypeStruct((ATT_HEADS * V_ROWS, s), BF16),
            jax.ShapeDtypeStruct((IDX_HEADS, s), F32),
        ],
        scratch_shapes=[pltpu.VMEM((512, tm), F32), pltpu.VMEM((LANES, tm), F32)],
        compiler_params=pltpu.CompilerParams(
            dimension_semantics=("parallel",), vmem_limit_bytes=VMEM_LIMIT),
        name="proj_rope",
    )(x2, wT, bT, cosT, sinT)


def _dsa_kernel(qT_ref, iqT_ref, iwT_ref, ik_ref, k_hbm, vT_hbm, o_ref,
                hi_ref, lo_ref, kbuf, vbuf, sem, iqp_ref, qp_ref, m_ref, tmx_ref, acc_ref, out_ref,
                sa_ref, sb_ref, *, n_sel):
    tq = qT_ref.shape[1]
    tk = kbuf.shape[2]
    i = pl.program_id(0)
    q0 = i * tq
    n_kt = (q0 + tq + tk - 1) // tk

    zpad = jnp.zeros((LANES - HEAD_DIM, tq), BF16)
    for h in range(IDX_HEADS):
        iqp_ref[h, 0:HEAD_DIM, :] = iqT_ref[h * HEAD_DIM:(h + 1) * HEAD_DIM, :]
        iqp_ref[h, HEAD_DIM:, :] = zpad
    row = lax.broadcasted_iota(jnp.int32, (256, tq), 0)
    for h in range(ATT_HEADS):
        j, hh = h // 4, h % 4
        quad = qT_ref[256 * j:256 * (j + 1), :]
        keep = (row >= hh * HEAD_DIM) & (row < (hh + 1) * HEAD_DIM)
        qp_ref[h] = jnp.where(keep, quad, jnp.zeros_like(quad))

    q_chunk = (q0 + lax.broadcasted_iota(jnp.int32, (1, tq), 1)) // CHUNK

    def score_tile(kt, mask_chunks):
        ks = pl.multiple_of(kt * tk, tk)
        ikt = ik_ref[pl.ds(ks, tk), :]
        acc = jnp.zeros((tk, tq), F32)
        for h in range(IDX_HEADS):
            z = _dot(ikt, iqp_ref[h])
            acc = acc + jnp.maximum(z, 0.0) * iwT_ref[h:h + 1, :]
        acc = acc + 0.0
        bits = pltpu.bitcast(acc, jnp.int32)
        skey = bits ^ ((bits >> 31) & jnp.int32(0x7FFFFFFF))
        if mask_chunks:
            key_chunk = (ks + lax.broadcasted_iota(jnp.int32, (tk, 1), 0)) // CHUNK
            skey = jnp.where(key_chunk <= q_chunk, skey, jnp.int32(INT_MIN))
        hi_ref[pl.ds(ks, tk), :] = (skey >> 16).astype(jnp.int16)
        lo_ref[pl.ds(ks, tk), :] = ((skey & jnp.int32(0xFFFF)) - HALF_OFFSET).astype(jnp.int16)

    lax.fori_loop(0, n_kt - 1, lambda kt, c: (score_tile(kt, False), c)[1], 0)
    score_tile(n_kt - 1, True)

    one16, zero16 = jnp.int16(1), jnp.int16(0)

    def add_rows(c, flags):
        parts = [flags[16 * r:16 * (r + 1), :] for r in range(tk // 16)]
        while len(parts) > 1:
            parts = [a + b for a, b in zip(parts[0::2], parts[1::2])]
        return c + parts[0]

    def total(c):
        return jnp.sum(c.astype(jnp.int32), axis=0, keepdims=True)

    def select16(plane_ref, need, count_all):
        def step(b, carry):
            ans_u, cnt_ans = carry
            cand_u = ans_u | (jnp.int32(1) << (15 - b))
            cand = (cand_u - HALF_OFFSET).astype(jnp.int16)

            def body(kt, c):
                ks = pl.multiple_of(kt * tk, tk)
                return add_rows(c, jnp.where(plane_ref[pl.ds(ks, tk), :] >= cand, one16, zero16))

            cnt = total(lax.fori_loop(0, n_kt, body, jnp.zeros((16, tq), jnp.int16)))
            ok = cnt >= need
            return jnp.where(ok, cand_u, ans_u), jnp.where(ok, cnt, cnt_ans)

        return lax.fori_loop(0, 16, step, (jnp.zeros((1, tq), jnp.int32), count_all))

    zero_row = jnp.zeros((1, tq), jnp.int32)
    hi_u, _ = select16(hi_ref, n_sel, zero_row)
    hi_t = (hi_u - HALF_OFFSET).astype(jnp.int16)

    def restrict_low(kt, carry):
        c_gt, c_eq = carry
        ks = pl.multiple_of(kt * tk, tk)
        hi = hi_ref[pl.ds(ks, tk), :]
        at = hi == hi_t
        lo_ref[pl.ds(ks, tk), :] = jnp.where(at, lo_ref[pl.ds(ks, tk), :], jnp.int16(-HALF_OFFSET))
        return (add_rows(c_gt, jnp.where(hi > hi_t, one16, zero16)),
                add_rows(c_eq, jnp.where(at, one16, zero16)))

    zero_cnt = jnp.zeros((16, tq), jnp.int16)
    c_gt, c_eq = lax.fori_loop(0, n_kt, restrict_low, (zero_cnt, zero_cnt))
    n_above = total(c_gt)
    lo_u, n_at = select16(lo_ref, n_sel - n_above, total(c_eq))
    short = hi_u == 0
    lo_u = jnp.where(short & (lo_u == 0), 1, lo_u)
    lo_t = (lo_u - HALF_OFFSET).astype(jnp.int16)

    excess = jnp.where(short, 0, n_above + n_at - n_sel)

    @pl.when(jnp.max(excess) > 0)
    def _():
        def tied(kt):
            ks = pl.multiple_of(kt * tk, tk)
            eq = (hi_ref[pl.ds(ks, tk), :] == hi_t) & (lo_ref[pl.ds(ks, tk), :] == lo_t)
            return ks, jnp.where(eq, jnp.ones((), BF16), jnp.zeros((), BF16))

        def count_tied(kt, c):
            return c + jnp.sum(tied(kt)[1].astype(F32), axis=0, keepdims=True)

        n_tied = lax.fori_loop(0, n_kt, count_tied, jnp.zeros((1, tq), F32))
        keep = n_tied - excess.astype(F32)
        r_i = lax.broadcasted_iota(jnp.int32, (tk, tk), 0)
        c_i = lax.broadcasted_iota(jnp.int32, (tk, tk), 1)
        before = jnp.where(c_i < r_i, 1.0, 0.0).astype(BF16)

        def drop_late(kt, seen):
            ks, eq = tied(kt)
            rank = _dot(before, eq) + seen
            eq_f = eq.astype(F32)
            drop = jnp.where(rank >= keep, eq_f, 0.0).astype(BF16)
            gone = jnp.int16(-HALF_OFFSET)
            hi_ref[pl.ds(ks, tk), :] = jnp.where(drop > 0, gone, hi_ref[pl.ds(ks, tk), :])
            lo_ref[pl.ds(ks, tk), :] = jnp.where(drop > 0, gone, lo_ref[pl.ds(ks, tk), :])
            return seen + jnp.sum(eq_f, axis=0, keepdims=True)

        lax.fori_loop(0, n_kt, drop_late, jnp.zeros((1, tq), F32))

    n2 = (n_kt + 1) // 2
    last = 2 * n2 - 1

    @pl.when(n_kt % 2 == 1)
    def _():
        sentinel = jnp.full((tk, tq), -HALF_OFFSET, jnp.int16)
        hi_ref[pl.ds(pl.multiple_of(n_kt * tk, tk), tk), :] = sentinel
        lo_ref[pl.ds(pl.multiple_of(n_kt * tk, tk), tk), :] = sentinel

    def k_copy(t, par, w):
        return pltpu.make_async_copy(k_hbm.at[pl.ds(t * tk, tk), :], kbuf.at[par, w], sem.at[0, par, w])

    def v_copy(t, par, w):
        return pltpu.make_async_copy(vT_hbm.at[:, pl.ds(t * tk, tk)], vbuf.at[par, w], sem.at[1, par, w])

    def pair_copies(j, par):
        return (k_copy(2 * j + 1, par, 0), k_copy(jnp.minimum(2 * j + 2, last), par, 1),
                v_copy(2 * j, par, 0), v_copy(2 * j + 1, par, 1))

    def logits_head(kb, h, bias, s_dst, tmx_dst):
        j = h // 4
        s = _dot(kb[:, 256 * j:256 * (j + 1)], qp_ref[h]) + bias
        s_dst[h] = s
        tmx_dst[h:h + 1, :] = jnp.max(s, axis=0, keepdims=True)

    def mask_bias(t):
        ks = pl.multiple_of(t * tk, tk)
        hi = hi_ref[pl.ds(ks, tk), :]
        sel = (hi > hi_t) | ((hi == hi_t) & (lo_ref[pl.ds(ks, tk), :] >= lo_t))
        return jnp.where(sel, jnp.zeros((), BF16), jnp.full((), NEG_BIG, BF16)).astype(F32)

    def attend_head(vb, h, s_src, tmx_src):
        m_old = m_ref[h:h + 1, :]
        m_new = jnp.maximum(m_old, tmx_src[h:h + 1, :])
        alpha = jnp.exp2(m_old - m_new)
        p = jnp.exp2(s_src[h] - m_new)
        a0 = h * V_ROWS
        pv = _dot(vb[a0:a0 + V_ROWS, :], p.astype(BF16))
        acc_ref[a0:a0 + V_ROWS, :] = alpha * acc_ref[a0:a0 + V_ROWS, :] + pv
        m_ref[h:h + 1, :] = m_new

    def half_step(kb, t_next, s_dst, tmx_dst, vb, s_src, tmx_src):
        bias = mask_bias(t_next)
        for h in range(ATT_HEADS):
            logits_head(kb, h, bias, s_dst, tmx_dst)
            attend_head(vb, h, s_src, tmx_src)

    m_ref[...] = jnp.full(m_ref.shape, NEG_BIG, F32)
    acc_ref[...] = jnp.zeros(acc_ref.shape, F32)

    first = k_copy(0, 1, 1)
    first.start()
    for c in pair_copies(0, 0):
        c.start()
    first.wait()
    bias0 = mask_bias(0)
    for h in range(ATT_HEADS):
        logits_head(kbuf.at[1, 1], h, bias0, sa_ref, tmx_ref.at[0])

    def pair_step(j, carry):
        par = j % 2
        for c in pair_copies(j, par):
            c.wait()

        @pl.when(j + 1 < n2)
        def _():
            for c in pair_copies(j + 1, 1 - par):
                c.start()

        half_step(kbuf.at[par, 0], 2 * j + 1, sb_ref, tmx_ref.at[1], vbuf.at[par, 0], sa_ref, tmx_ref.at[0])
        half_step(kbuf.at[par, 1], jnp.minimum(2 * j + 2, last), sa_ref, tmx_ref.at[0],
                  vbuf.at[par, 1], sb_ref, tmx_ref.at[1])
        return carry

    lax.fori_loop(0, n2, pair_step, 0)

    for h in range(ATT_HEADS):
        a0 = h * V_ROWS
        den = acc_ref[a0 + HEAD_DIM:a0 + HEAD_DIM + 1, :]
        out_ref[h * HEAD_DIM:(h + 1) * HEAD_DIM, :] = acc_ref[a0:a0 + HEAD_DIM, :] / den
    o_ref[...] = out_ref[...].T.astype(BF16)


def _dsa(qT, iqT, iwT, ik, k, vT):
    s = k.shape[0]
    tq = min(DSA_TQ, s)
    tk = min(DSA_TK, s // 2)
    assert s % (2 * tk) == 0 and s % tq == 0 and tk % tq == 0
    n_sel = min(TOPK_MAX, s // 4)
    col = lambda i: (0, i)
    return pl.pallas_call(
        functools.partial(_dsa_kernel, n_sel=n_sel),
        grid=(s // tq,),
        in_specs=[
            pl.BlockSpec((512, tq), col),
            pl.BlockSpec((512, tq), col),
            pl.BlockSpec((IDX_HEADS, tq), col),
            pl.BlockSpec((s, LANES), lambda i: (0, 0)),
            pl.BlockSpec(memory_space=pl.ANY),
            pl.BlockSpec(memory_space=pl.ANY),
        ],
        out_specs=pl.BlockSpec((tq, 512), lambda i: (i, 0)),
        out_shape=jax.ShapeDtypeStruct((s, 512), BF16),
        scratch_shapes=[
            pltpu.VMEM((s, tq), jnp.int16),
            pltpu.VMEM((s, tq), jnp.int16),
            pltpu.VMEM((2, 2, tk, 512), BF16),
            pltpu.VMEM((2, 2, ATT_HEADS * V_ROWS, tk), BF16),
            pltpu.SemaphoreType.DMA((2, 2, 2)),
            pltpu.VMEM((IDX_HEADS, LANES, tq), BF16),
            pltpu.VMEM((ATT_HEADS, 256, tq), BF16),
            pltpu.VMEM((ATT_HEADS, tq), F32),
            pltpu.VMEM((2, ATT_HEADS, tq), F32),
            pltpu.VMEM((ATT_HEADS * V_ROWS, tq), F32),
            pltpu.VMEM((512, tq), F32),
            pltpu.VMEM((ATT_HEADS, tk, tq), F32),
            pltpu.VMEM((ATT_HEADS, tk, tq), F32),
        ],
        compiler_params=pltpu.CompilerParams(
            dimension_semantics=("arbitrary",), vmem_limit_bytes=VMEM_LIMIT),
        name="dsa",
    )(qT, iqT, iwT, ik, k, vT)


_C_POOL, _C_MQ, _C_GATE = 0, 512, 1024


def _route(logitsT, bias):
    sc = _sigmoid(logitsT)
    bi = sc + bias
    s_rows = [sc[e:e + 1, :] for e in range(N_EXPERTS)]
    b_rows = [bi[e:e + 1, :] for e in range(N_EXPERTS)]
    best = None
    best_score = None
    for g in range(N_GROUPS):
        v = b_rows[4 * g:4 * g + 4]
        top2 = None
        for a in range(4):
            for c in range(a + 1, 4):
                pair = v[a] + v[c]
                top2 = pair if top2 is None else jnp.maximum(top2, pair)
        if best is None:
            best, best_score = jnp.zeros_like(top2, dtype=jnp.int32), top2
        else:
            upd = top2 > best_score
            best = jnp.where(upd, g, best)
            best_score = jnp.where(upd, top2, best_score)
    wb, ws = [], []
    for m in range(EXPERTS_PER_GROUP):
        vb, vs = b_rows[m], s_rows[m]
        for g in range(1, N_GROUPS):
            vb = jnp.where(best == g, b_rows[4 * g + m], vb)
            vs = jnp.where(best == g, s_rows[4 * g + m], vs)
        wb.append(vb)
        ws.append(vs)
    i1, v1 = jnp.zeros_like(best), wb[0]
    for m in range(1, 4):
        upd = wb[m] > v1
        i1 = jnp.where(upd, m, i1)
        v1 = jnp.where(upd, wb[m], v1)
    i2, v2 = None, None
    for m in range(4):
        cand = jnp.where(i1 == m, -jnp.inf, wb[m])
        if i2 is None:
            i2, v2 = jnp.zeros_like(best), cand
        else:
            upd = cand > v2
            i2 = jnp.where(upd, m, i2)
            v2 = jnp.where(upd, cand, v2)
    s1 = ws[0]
    s2 = ws[0]
    for m in range(1, 4):
        s1 = jnp.where(i1 == m, ws[m], s1)
        s2 = jnp.where(i2 == m, ws[m], s2)
    tot = s1 + s2
    g1, g2 = s1 / tot, s2 / tot
    rows = []
    for e in range(N_EXPERTS):
        g, m = e // 4, e % 4
        val = jnp.where(i1 == m, g1, jnp.where(i2 == m, g2, 0.0))
        rows.append(jnp.where(best == g, val, 0.0))
    return rows


def _mix_kernel(x_ref, xh_ref, dsa_ref, w_ref, b_ref, pw_ref, ps_ref, mk_ref, mv_ref,
                wbr_ref, wout_ref, g_ref, beta_ref, wr_ref, rb_ref,
                x1_ref, gate_ref, uext_ref):
    i = pl.program_id(0)
    tm = x_ref.shape[0]
    x = x_ref[...]
    xb = x.astype(BF16)

    def proj(c0, c1):
        return _dot(xb, w_ref[:, c0:c1]) + b_ref[:, c0:c1]

    u = proj(_C_POOL, _C_MQ)
    uh = _dot(xh_ref[...].astype(BF16), w_ref[:, _C_POOL:_C_MQ]) + b_ref[:, _C_POOL:_C_MQ]
    uext_ref[0:POOL_HALO, :] = jnp.where(i > 0, uh, 0.0)
    uext_ref[POOL_HALO:, :] = u
    t = i * tm + lax.broadcasted_iota(jnp.int32, (tm, 1), 0)
    pooled = []
    for g, w in enumerate(POOL_WINDOWS):
        c0 = g * POOL_GROUP_DIM
        ug = u[:, c0:c0 + POOL_GROUP_DIM]
        win = ug
        for j in range(1, w):
            win = win + uext_ref[POOL_HALO - j:POOL_HALO - j + tm, c0:c0 + POOL_GROUP_DIM]
        cnt = jnp.minimum(t + 1, w).astype(F32)
        d = win / cnt - ug
        pooled.append(_dot(d.astype(BF16), pw_ref[g]))
    pool_out = jnp.concatenate(pooled, axis=-1) * ps_ref[...]

    mq = proj(_C_MQ, _C_GATE)
    mem = []
    for h in range(MEM_HEADS):
        c0 = h * MEM_HEAD_DIM
        s = _dot_nt(mq[:, c0:c0 + MEM_HEAD_DIM].astype(BF16), mk_ref[:, c0:c0 + MEM_HEAD_DIM])
        s = s * (MEM_HEAD_DIM ** -0.5)
        s = s - jnp.max(s, axis=-1, keepdims=True)
        p = jnp.exp(s)
        p = p / jnp.sum(p, axis=-1, keepdims=True)
        mem.append(_dot(p.astype(BF16), mv_ref[:, c0:c0 + MEM_HEAD_DIM]))
    mem_out = jnp.concatenate(mem, axis=-1)

    merged = jnp.zeros((tm, D_MODEL), F32)
    for n, br in enumerate((pool_out.astype(BF16), dsa_ref[...], mem_out.astype(BF16))):
        gate = _sigmoid(proj(_C_GATE + n * D_MODEL, _C_GATE + (n + 1) * D_MODEL))
        merged = merged + gate * _dot(br, wbr_ref[n])
    mix = _dot(merged.astype(BF16), wout_ref[...])
    x1 = _layer_norm(DN_ALPHA * x + mix, g_ref[...], beta_ref[...])
    x1_ref[...] = x1

    logitsT = _dot_nt(wr_ref[...], x1.astype(BF16))
    rows = _route(logitsT, rb_ref[...])
    gT = jnp.concatenate(rows + [jnp.zeros((LANES - N_EXPERTS, tm), F32)], axis=0)
    gate_ref[...] = gT.T


def _mix(x2, dsa, w, b, pw, ps, mk, mv, wbr, wout, g, beta, wr, rb):
    s = x2.shape[0]
    tm = min(MIX_TM, s)
    const2 = lambda i: (0, 0)
    const3 = lambda i: (0, 0, 0)
    hb = tm // POOL_HALO
    return pl.pallas_call(
        _mix_kernel,
        grid=(s // tm,),
        in_specs=[
            pl.BlockSpec((tm, D_MODEL), lambda i: (i, 0)),
            pl.BlockSpec((POOL_HALO, D_MODEL), lambda i: (jnp.maximum(i * hb - 1, 0), 0)),
            pl.BlockSpec((tm, BRANCH_WIDTH), lambda i: (i, 0)),
            pl.BlockSpec(w.shape, const2),
            pl.BlockSpec(b.shape, const2),
            pl.BlockSpec(pw.shape, const3),
            pl.BlockSpec(ps.shape, const2),
            pl.BlockSpec(mk.shape, const2),
            pl.BlockSpec(mv.shape, const2),
            pl.BlockSpec(wbr.shape, const3),
            pl.BlockSpec(wout.shape, const2),
            pl.BlockSpec(g.shape, const2),
            pl.BlockSpec(beta.shape, const2),
            pl.BlockSpec(wr.shape, const2),
            pl.BlockSpec(rb.shape, const2),
        ],
        out_specs=[
            pl.BlockSpec((tm, D_MODEL), lambda i: (i, 0)),
            pl.BlockSpec((tm, LANES), lambda i: (i, 0)),
        ],
        out_shape=[
            jax.ShapeDtypeStruct((s, D_MODEL), F32),
            jax.ShapeDtypeStruct((s, LANES), F32),
        ],
        scratch_shapes=[pltpu.VMEM((POOL_HALO + tm, BRANCH_WIDTH), F32)],
        compiler_params=pltpu.CompilerParams(
            dimension_semantics=("parallel",), vmem_limit_bytes=VMEM_LIMIT),
        name="mix",
    )(x2, x2, dsa, w, b, pw, ps, mk, mv, wbr, wout, g, beta, wr, rb)


def _moe_kernel(x_ref, gate_ref, w1_ref, w3_ref, w2_ref, g_ref, beta_ref, o_ref, xb_ref, acc_ref):
    e = pl.program_id(1)

    @pl.when(e == 0)
    def _():
        xb_ref[...] = x_ref[...].astype(BF16)
        acc_ref[...] = jnp.zeros(acc_ref.shape, F32)

    xb = xb_ref[...]
    h1 = _dot(xb, w1_ref[0].astype(BF16))
    h3 = _dot(xb, w3_ref[0].astype(BF16))
    h = (h1 * _sigmoid(h1)) * h3
    y = _dot(h.astype(BF16), w2_ref[0].astype(BF16))
    lane = lax.broadcasted_iota(jnp.int32, gate_ref.shape, 1)
    ge = jnp.sum(jnp.where(lane == e, gate_ref[...], 0.0), axis=-1, keepdims=True)
    acc_ref[...] += y * ge

    @pl.when(e == N_EXPERTS - 1)
    def _():
        o_ref[...] = _layer_norm(DN_ALPHA * x_ref[...] + acc_ref[...], g_ref[...], beta_ref[...])


def _moe(x1, gates, w1, w3, w2, layer, g, beta):
    s = x1.shape[0]
    tm = min(MOE_TM, s)
    const2 = lambda i, e: (0, 0)
    expert = lambda i, e: (layer, e, 0, 0)
    return pl.pallas_call(
        _moe_kernel,
        grid=(s // tm, N_EXPERTS),
        in_specs=[
            pl.BlockSpec((tm, D_MODEL), lambda i, e: (i, 0)),
            pl.BlockSpec((tm, LANES), lambda i, e: (i, 0)),
            pl.BlockSpec((None, 1, D_MODEL, D_EXPERT), expert),
            pl.BlockSpec((None, 1, D_MODEL, D_EXPERT), expert),
            pl.BlockSpec((None, 1, D_EXPERT, D_MODEL), expert),
            pl.BlockSpec(g.shape, const2),
            pl.BlockSpec(beta.shape, const2),
        ],
        out_specs=pl.BlockSpec((tm, D_MODEL), lambda i, e: (i, 0)),
        out_shape=jax.ShapeDtypeStruct((s, D_MODEL), F32),
        scratch_shapes=[pltpu.VMEM((tm, D_MODEL), BF16), pltpu.VMEM((tm, D_MODEL), F32)],
        compiler_params=pltpu.CompilerParams(
            dimension_semantics=("parallel", "arbitrary"), vmem_limit_bytes=VMEM_LIMIT),
        name="moe",
    )(x1, gates, w1, w3, w2, g, beta)


def _split_w_in(w_in, b_in):
    o = [0, 512, 1024, 1536, 2048, 2560, 2624, 2632, 3144, 6216]
    pool, q, k, v, iq, ik, iw, mq, gates = [(w_in[:, a:c], b_in[a:c]) for a, c in zip(o[:-1], o[1:])]
    qs = HEAD_DIM ** -0.5 * LOG2_E
    iws = (HEAD_DIM ** -0.5) * (IDX_HEADS ** -0.5)
    pad = _R_END - _R_IW - IDX_HEADS
    wT = jnp.concatenate([q[0] * qs, iq[0], k[0], ik[0], v[0], iw[0] * iws,
                          jnp.zeros((D_MODEL, pad), F32)], axis=1).T
    bT = jnp.concatenate([q[1] * qs, iq[1], k[1], ik[1], v[1], iw[1] * iws, jnp.zeros((pad,), F32)])
    w_mix = jnp.concatenate([pool[0], mq[0], gates[0]], axis=1)
    b_mix = jnp.concatenate([pool[1], mq[1], gates[1]])
    return wT.astype(BF16), bT[:, None], w_mix.astype(BF16), b_mix[None, :]


def kernel(x, mem, positions, w_in, b_in, pool_w, pool_scale, w_mem_kv, w_br, w_out, ln1_g, ln1_b,
           w_router, router_bias, w1, w3, w2, ln2_g, ln2_b):
    B, S, D = x.shape
    inv_freq = ROPE_THETA ** (-jnp.arange(ROPE_HALF, dtype=F32) / ROPE_HALF)
    wr = w_router.T.astype(BF16)
    rb = router_bias.astype(F32)[:, None]
    outs = []
    for bi in range(B):
        xs = x[bi]
        ang = positions[bi].astype(F32)[None, :] * inv_freq[:, None]
        cosT, sinT = jnp.cos(ang), jnp.sin(ang)
        for l in range(DEPTH):
            wT, bT, w_mix, b_mix = _split_w_in(w_in[l], b_in[l])
            kv = _matmul(mem[bi], w_mem_kv[l].astype(BF16))
            mk = kv[:, :BRANCH_WIDTH].astype(BF16)
            mv = kv[:, BRANCH_WIDTH:].astype(BF16)
            qT, iqT, k, ik, vT, iwT = _proj_rope(xs, wT, bT, cosT, sinT)
            dsa = _dsa(qT, iqT, iwT, ik, k, vT)
            x1, gates = _mix(xs, dsa, w_mix, b_mix, pool_w[l].astype(BF16), pool_scale[l][None, :],
                             mk, mv, w_br[l].astype(BF16), w_out[l].astype(BF16),
                             ln1_g[l][None, :], ln1_b[l][None, :], wr, rb)
            xs = _moe(x1, gates, w1, w3, w2, l, ln2_g[l][None, :], ln2_b[l][None, :])
        outs.append(xs)
    return jnp.stack(outs, axis=0)
```
